```python
import math
import jax, jax.numpy as jnp
from jax import lax
import numpy as np

D_MODEL = 1024
BATCH = 8
SEQ = 4096
DEPTH = 4
DEC_BATCH = 8
DEC_SEQ = 32
PAST_LEN = 2048

CHUNK = 64
Q_BLOCK = 128
N_MIXERS = 2
SB_HEADS = 16
SB_HEAD_DIM = D_MODEL // SB_HEADS
DIFF_HEADS = 8
DIFF_HEAD_DIM = D_MODEL // (2 * DIFF_HEADS)
ROPE_DIM = DIFF_HEAD_DIM // 4
ROPE_THETA = 500000.0
D_FF = 4 * D_MODEL
LN_EPS = 1e-5
SUBLN_EPS = 1e-5
DEEPNORM_ALPHA = (2 * DEPTH) ** 0.25
DEEPNORM_BETA = (8 * DEPTH) ** -0.25
N_SB_LAYERS = (DEPTH + 1) // 2
N_DIFF_LAYERS = DEPTH // 2
NEG_INF = float(np.finfo(np.float32).min)

kernel_name = "stickbreak_diffattn_deepnorm_stream_step"


def _layer_norm(x, g, b):
    xf = x.astype(jnp.float32)
    mu = jnp.mean(xf, axis=-1, keepdims=True)
    var = jnp.mean(jnp.square(xf - mu), axis=-1, keepdims=True)
    return ((xf - mu) * lax.rsqrt(var + LN_EPS) * g + b).astype(x.dtype)


def _rope_partial(x, pos):
    half = ROPE_DIM // 2
    inv_freq = ROPE_THETA ** (-jnp.arange(0, ROPE_DIM, 2, dtype=jnp.float32) / ROPE_DIM)
    ang = pos.astype(jnp.float32)[:, None] * inv_freq[None, :]
    cos = jnp.cos(ang)[None, :, None, None, :]
    sin = jnp.sin(ang)[None, :, None, None, :]
    xr = x[..., :ROPE_DIM].astype(jnp.float32)
    x1, x2 = xr[..., :half], xr[..., half:]
    rot = jnp.concatenate([x1 * cos - x2 * sin, x2 * cos + x1 * sin], axis=-1)
    return jnp.concatenate([rot.astype(x.dtype), x[..., ROPE_DIM:]], axis=-1)


def _sweep_query_blocks(fn, q, q_pos):
    B, S = q.shape[0], q.shape[1]
    qb = min(Q_BLOCK, S)
    nb = S // qb
    if nb == 1:
        return fn(q, q_pos)
    qs = jnp.moveaxis(q.reshape((B, nb, qb) + q.shape[2:]), 1, 0)
    ps = q_pos.reshape(nb, qb)
    out = lax.map(lambda a: fn(a[0], a[1]), (qs, ps))
    out = jnp.moveaxis(out, 0, 1)
    return out.reshape((B, S) + out.shape[3:])


def _stick_breaking_block(q, q_pos, k, v, k_pos):
    z = jnp.einsum('bqhd,bkhd->bhqk', q, k, preferred_element_type=jnp.float32) * (SB_HEAD_DIM ** -0.5)
    mask = k_pos[None, :] < q_pos[:, None]
    log_keep = jnp.where(mask, jax.nn.log_sigmoid(-z), 0.0)
    after = lax.cumsum(log_keep, axis=3, reverse=True) - log_keep
    w = jnp.where(mask, jnp.exp(jax.nn.log_sigmoid(z) + after), 0.0)
    return jnp.einsum('bhqk,bkhd->bqhd', w.astype(v.dtype), v)


def _diff_block(q, q_pos, k, v, k_pos, lam):
    s = jnp.einsum('bqhcd,bkhcd->bhcqk', q, k, preferred_element_type=jnp.float32) * (DIFF_HEAD_DIM ** -0.5)
    mask = (k_pos[None, :] // CHUNK) <= (q_pos[:, None] // CHUNK)
    p = jax.nn.softmax(jnp.where(mask, s, NEG_INF), axis=-1)
    a = p[:, :, 0] - lam * p[:, :, 1]
    return jnp.einsum('bhqk,bkhe->bqhe', a.astype(v.dtype), v)


def _stick_breaking_mixer(x, pos, k_past, v_past, w_qkv, w_o):
    B, S, _ = x.shape
    qkv = (x @ w_qkv).reshape(B, S, 3, SB_HEADS, SB_HEAD_DIM)
    q, k, v = qkv[:, :, 0], qkv[:, :, 1], qkv[:, :, 2]
    if k_past is None:
        k_all, v_all, k_pos = k, v, pos
    else:
        k_all = jnp.concatenate([k_past.astype(k.dtype), k], axis=1)
        v_all = jnp.concatenate([v_past.astype(v.dtype), v], axis=1)
        k_pos = jnp.concatenate([jnp.arange(k_past.shape[1], dtype=jnp.int32), pos])
    o = _sweep_query_blocks(lambda qb, pb: _stick_breaking_block(qb, pb, k_all, v_all, k_pos), q, pos)
    return o.reshape(B, S, D_MODEL) @ w_o, k, v


def _diff_mixer(x, pos, k_past, v_past, w_qkv, lq1, lk1, lq2, lk2, subln_g, w_o, lambda_init):
    B, S, _ = x.shape
    qkv = (x @ w_qkv).reshape(B, S, 3, DIFF_HEADS, 2, DIFF_HEAD_DIM)
    q = _rope_partial(qkv[:, :, 0], pos)
    k = _rope_partial(qkv[:, :, 1], pos)
    v = qkv[:, :, 2].reshape(B, S, DIFF_HEADS, 2 * DIFF_HEAD_DIM)
    lam = (jnp.exp(jnp.sum(lq1.astype(jnp.float32) * lk1.astype(jnp.float32)))
           - jnp.exp(jnp.sum(lq2.astype(jnp.float32) * lk2.astype(jnp.float32))) + lambda_init)
    if k_past is None:
        k_all, v_all, k_pos = k, v, pos
    else:
        k_all = jnp.concatenate([k_past.astype(k.dtype), k], axis=1)
        v_all = jnp.concatenate([v_past.astype(v.dtype), v], axis=1)
        k_pos = jnp.concatenate([jnp.arange(k_past.shape[1], dtype=jnp.int32), pos])
    o = _sweep_query_blocks(lambda qb, pb: _diff_block(qb, pb, k_all, v_all, k_pos, lam), q, pos)
    of = o.astype(jnp.float32)
    of = of * lax.rsqrt(jnp.mean(jnp.square(of), axis=-1, keepdims=True) + SUBLN_EPS) * subln_g
    o = (of * (1.0 - lambda_init)).astype(x.dtype)
    return o.reshape(B, S, D_MODEL) @ w_o, k, v


def _sq_relu_mlp(x, w_up, w_down):
    h = jax.nn.relu(x @ w_up)
    return (h * h) @ w_down


def _trunk(x, pos, sb_k_past, sb_v_past, diff_k_past, diff_v_past,
           sb_w_qkv, sb_w_o, diff_w_qkv, diff_lambda_q1, diff_lambda_k1, diff_lambda_q2, diff_lambda_k2,
           diff_subln_g, diff_w_o, ln1_g, ln1_b, mlp_w_up, mlp_w_down, ln2_g, ln2_b):
    sb_k, sb_v, df_k, df_v = [], [], [], []
    for i in range(DEPTH):
        j = i // N_MIXERS
        if i % N_MIXERS == 0:
            kp = None if sb_k_past is None else sb_k_past[j]
            vp = None if sb_v_past is None else sb_v_past[j]
            mix, kn, vn = _stick_breaking_mixer(x, pos, kp, vp, sb_w_qkv[j], sb_w_o[j])
            sb_k.append(kn)
            sb_v.append(vn)
        else:
            kp = None if diff_k_past is None else diff_k_past[j]
            vp = None if diff_v_past is None else diff_v_past[j]
            lambda_init = 0.8 - 0.6 * math.exp(-0.3 * i)
            mix, kn, vn = _diff_mixer(x, pos, kp, vp, diff_w_qkv[j], diff_lambda_q1[j], diff_lambda_k1[j],
                                      diff_lambda_q2[j], diff_lambda_k2[j], diff_subln_g[j], diff_w_o[j],
                                      lambda_init)
            df_k.append(kn)
            df_v.append(vn)
        x = _layer_norm(DEEPNORM_ALPHA * x + mix, ln1_g[i], ln1_b[i])
        x = _layer_norm(DEEPNORM_ALPHA * x + _sq_relu_mlp(x, mlp_w_up[i], mlp_w_down[i]), ln2_g[i], ln2_b[i])
    return x, jnp.stack(sb_k), jnp.stack(sb_v), jnp.stack(df_k), jnp.stack(df_v)


def setup_inputs(seed: int = 0) -> dict:
    key = jax.random.key(seed)
    ks = jax.random.split(key, 24)
    D = D_MODEL

    def nrm(k, shape, scale):
        return scale * jax.random.normal(k, shape, jnp.float32)

    w_in = D ** -0.5
    return {
        "x_prompt": nrm(ks[0], (BATCH, SEQ, D), 1.0),
        "x_sample": nrm(ks[1], (DEC_BATCH, DEC_SEQ, D), 1.0),
        "cache_sb_k": nrm(ks[2], (N_SB_LAYERS, DEC_BATCH, PAST_LEN, SB_HEADS, SB_HEAD_DIM), 1.0),
        "cache_sb_v": nrm(ks[3], (N_SB_LAYERS, DEC_BATCH, PAST_LEN, SB_HEADS, SB_HEAD_DIM), DEEPNORM_BETA),
        "cache_diff_k": nrm(ks[4], (N_DIFF_LAYERS, DEC_BATCH, PAST_LEN, DIFF_HEADS, 2, DIFF_HEAD_DIM), 1.0),
        "cache_diff_v": nrm(ks[5], (N_DIFF_LAYERS, DEC_BATCH, PAST_LEN, DIFF_HEADS, 2 * DIFF_HEAD_DIM), DEEPNORM_BETA),
        "sb_w_qkv": jnp.concatenate([nrm(ks[6], (N_SB_LAYERS, D, 2 * D), w_in),
                                     nrm(ks[7], (N_SB_LAYERS, D, D), DEEPNORM_BETA * w_in)], axis=-1),
        "sb_w_o": nrm(ks[8], (N_SB_LAYERS, D, D), DEEPNORM_BETA * w_in),
        "diff_w_qkv": jnp.concatenate([nrm(ks[9], (N_DIFF_LAYERS, D, 2 * D), w_in),
                                       nrm(ks[10], (N_DIFF_LAYERS, D, D), DEEPNORM_BETA * w_in)], axis=-1),
        "diff_lambda_q1": nrm(ks[11], (N_DIFF_LAYERS, DIFF_HEAD_DIM), 0.1),
        "diff_lambda_k1": nrm(ks[12], (N_DIFF_LAYERS, DIFF_HEAD_DIM), 0.1),
        "diff_lambda_q2": nrm(ks[13], (N_DIFF_LAYERS, DIFF_HEAD_DIM), 0.1),
        "diff_lambda_k2": nrm(ks[14], (N_DIFF_LAYERS, DIFF_HEAD_DIM), 0.1),
        "diff_subln_g": 1.0 + nrm(ks[15], (N_DIFF_LAYERS, 2 * DIFF_HEAD_DIM), 0.02),
        "diff_w_o": nrm(ks[16], (N_DIFF_LAYERS, D, D), DEEPNORM_BETA * w_in),
        "ln1_g": 1.0 + nrm(ks[17], (DEPTH, D), 0.02),
        "ln1_b": nrm(ks[18], (DEPTH, D), 0.02),
        "mlp_w_up": nrm(ks[19], (DEPTH, D, D_FF), w_in),
        "mlp_w_down": nrm(ks[20], (DEPTH, D_FF, D), DEEPNORM_BETA * D_FF ** -0.5),
        "ln2_g": 1.0 + nrm(ks[21], (DEPTH, D), 0.02),
        "ln2_b": nrm(ks[22], (DEPTH, D), 0.02),
    }


def reference(x_prompt, x_sample, cache_sb_k, cache_sb_v, cache_diff_k, cache_diff_v,
              sb_w_qkv, sb_w_o, diff_w_qkv, diff_lambda_q1, diff_lambda_k1, diff_lambda_q2, diff_lambda_k2,
              diff_subln_g, diff_w_o, ln1_g, ln1_b, mlp_w_up, mlp_w_down, ln2_g, ln2_b):
    weights = (sb_w_qkv, sb_w_o, diff_w_qkv, diff_lambda_q1, diff_lambda_k1, diff_lambda_q2, diff_lambda_k2,
               diff_subln_g, diff_w_o, ln1_g, ln1_b, mlp_w_up, mlp_w_down, ln2_g, ln2_b)
    pos_p = jnp.arange(x_prompt.shape[1], dtype=jnp.int32)
    y_prompt, sbk_p, sbv_p, dfk_p, dfv_p = _trunk(x_prompt, pos_p, None, None, None, None, *weights)
    past_len = cache_sb_k.shape[2]
    pos_s = past_len + jnp.arange(x_sample.shape[1], dtype=jnp.int32)
    y_sample, sbk_s, sbv_s, dfk_s, dfv_s = _trunk(x_sample, pos_s, cache_sb_k, cache_sb_v,
                                                  cache_diff_k, cache_diff_v, *weights)
    return (y_prompt, y_sample, sbk_p, sbv_p, dfk_p, dfv_p, sbk_s, sbv_s, dfk_s, dfv_s)
```

```python
import functools
import math

import jax
import jax.numpy as jnp
from jax import lax
from jax.experimental import pallas as pl
from jax.experimental.pallas import tpu as pltpu

D_MODEL = 1024
DEPTH = 4
CHUNK = 64
N_MIXERS = 2
SB_HEADS = 16
SB_HEAD_DIM = D_MODEL // SB_HEADS
DIFF_HEADS = 8
DIFF_HEAD_DIM = D_MODEL // (2 * DIFF_HEADS)
ROPE_DIM = DIFF_HEAD_DIM // 4
ROPE_THETA = 500000.0
D_FF = 4 * D_MODEL
LN_EPS = 1e-5
SUBLN_EPS = 1e-5
DEEPNORM_ALPHA = (2 * DEPTH) ** 0.25

LANES = 128
ATT_BLOCK = 256
GROUPS_PER_STEP = 2
VMEM_LIMIT_BYTES = 56 * 1024 * 1024
LOG2E = 1.4426950408889634
LN2 = 0.6931471805599453
MASK_VALUE = -1e30

F32 = jnp.float32
BF16 = jnp.bfloat16


def _compiler_params(n_grid_dims):
    return pltpu.CompilerParams(
        dimension_semantics=("parallel",) * n_grid_dims,
        vmem_limit_bytes=VMEM_LIMIT_BYTES,
    )


def _resident(shape):
    return pl.BlockSpec(shape, lambda *_: (0,) * len(shape), pipeline_mode=pl.Buffered(1))


def _dot_nt(a, b):
    return lax.dot_general(a, b, (((1,), (1,)), ((), ())), preferred_element_type=F32)


QKV_COL_CHUNK = 512


def _qkv_kernel(*refs, rope, q_scale):
    if rope:
        x_ref, w_ref, cos_ref, sin_up_ref, sin_dn_ref, kf_ref, vf_ref, qb_ref, kb_ref, vb_ref = refs
        cos, sin_up, sin_dn = cos_ref[...], sin_up_ref[...], sin_dn_ref[...]
    else:
        x_ref, w_ref, kf_ref, vf_ref, qb_ref, kb_ref, vb_ref = refs
    xb = x_ref[...].astype(BF16)
    for part in range(3):
        for c in range(D_MODEL // QKV_COL_CHUNK):
            lo = c * QKV_COL_CHUNK
            cols = slice(lo, lo + QKV_COL_CHUNK)
            y = jnp.dot(xb, w_ref[:, part * D_MODEL + lo: part * D_MODEL + lo + QKV_COL_CHUNK],
                        preferred_element_type=F32)
            if rope and part < 2:
                pieces = []
                for j in range(QKV_COL_CHUNK // LANES):
                    yj = y[:, j * LANES:(j + 1) * LANES]
                    pieces.append(yj * cos
                                  + pltpu.roll(yj, LANES - ROPE_DIM // 2, 1) * sin_up
                                  + pltpu.roll(yj, ROPE_DIM // 2, 1) * sin_dn)
                y = jnp.concatenate(pieces, axis=1)
            if part == 0:
                qb_ref[:, cols] = (y * q_scale).astype(BF16)
            elif part == 1:
                kf_ref[:, cols] = y
                kb_ref[:, cols] = y.astype(BF16)
            else:
                vf_ref[:, cols] = y
                vb_ref[:, cols] = y.astype(BF16)


def _qkv_proj(x, w, rope_tables, q_scale, tm):
    n = x.shape[0]
    rope = rope_tables is not None
    row_spec = pl.BlockSpec((tm, D_MODEL), lambda i: (i, 0))
    in_specs = [row_spec, _resident((D_MODEL, 3 * D_MODEL))]
    args = [x, w]
    if rope:
        n_pos_tiles = rope_tables[0].shape[0] // tm
        tab_spec = pl.BlockSpec((tm, LANES), lambda i: (i % n_pos_tiles, 0))
        in_specs += [tab_spec] * 3
        args += list(rope_tables)
    out_shape = [jax.ShapeDtypeStruct((n, D_MODEL), F32)] * 2 + [jax.ShapeDtypeStruct((n, D_MODEL), BF16)] * 3
    return pl.pallas_call(
        functools.partial(_qkv_kernel, rope=rope, q_scale=q_scale),
        grid=(n // tm,),
        in_specs=in_specs,
        out_specs=[row_spec] * 5,
        out_shape=out_shape,
        compiler_params=_compiler_params(1),
        name="qkv_rope" if rope else "qkv",
    )(*args)


def _rope_tables(pos):
    half = ROPE_DIM // 2
    inv_freq = ROPE_THETA ** (-jnp.arange(0, ROPE_DIM, 2, dtype=F32) / ROPE_DIM)
    ang = pos.astype(F32)[:, None] * inv_freq[None, :]
    cos, sin = jnp.cos(ang), jnp.sin(ang)
    ones = jnp.ones((pos.shape[0], DIFF_HEAD_DIM - ROPE_DIM), F32)
    zeros_half = jnp.zeros_like(sin)
    zeros_rest = jnp.zeros_like(ones)
    cos_t = jnp.concatenate([cos, cos, ones], axis=1)
    sin_up = jnp.concatenate([-sin, zeros_half, zeros_rest], axis=1)
    sin_dn = jnp.concatenate([zeros_half, sin, zeros_rest], axis=1)
    reps = LANES // DIFF_HEAD_DIM
    return tuple(jnp.tile(t, (1, reps)) for t in (cos_t, sin_up, sin_dn))


def _neg_abs(z):
    bits = lax.bitcast_convert_type(z, jnp.uint32) | jnp.uint32(0x80000000)
    return lax.bitcast_convert_type(bits, F32)


def _sb_attn_kernel(q_ref, kd_ref, vd_ref, kp_ref, vp_ref, u_ref, o_ref, *, q_block_offset):
    n_past = pl.program_id(2) + q_block_offset
    lane = lax.broadcasted_iota(jnp.int32, (1, LANES), 1)
    row = lax.broadcasted_iota(jnp.int32, (ATT_BLOCK, ATT_BLOCK), 0)
    col = lax.broadcasted_iota(jnp.int32, (ATT_BLOCK, ATT_BLOCK), 1)
    causal = col < row
    u = u_ref[...]
    heads_per_group = LANES // SB_HEAD_DIM
    in_head = [(lane >= h * SB_HEAD_DIM) & (lane < (h + 1) * SB_HEAD_DIM) for h in range(heads_per_group)]

    def group_slice(g):
        return slice(g * LANES, (g + 1) * LANES)

    qm = []
    for g in range(GROUPS_PER_STEP):
        q = q_ref[0, :, group_slice(g)]
        qm.append([jnp.where(m, q, jnp.zeros_like(q)) for m in in_head])

    def group_tile(g, k, v, carries, masked):
        weights, v_parts, new_carries = [], [], []
        for h in range(heads_per_group):
            z = _dot_nt(qm[g][h], k)
            l = jnp.maximum(z, 0.0) + jnp.log(1.0 + jnp.exp2(_neg_abs(z))) * LOG2E
            if masked:
                l = jnp.where(causal, l, 0.0)
            c = jnp.dot(l.astype(BF16), u, preferred_element_type=F32)
            if carries is not None:
                c = c + carries[h]
            a = jnp.exp2(z - c)
            if masked:
                a = jnp.where(causal, a, 0.0)
            weights.append(a.astype(BF16))
            v_parts.append(jnp.where(in_head[h], v, jnp.zeros_like(v)))
            new_carries.append(c[:, 0:1])
        pv = jnp.dot(jnp.concatenate(weights, axis=1), jnp.concatenate(v_parts, axis=0),
                     preferred_element_type=F32)
        return pv, new_carries

    state = []
    for g in range(GROUPS_PER_STEP):
        pv, carries = group_tile(g, kd_ref[0, :, group_slice(g)], vd_ref[0, :, group_slice(g)], None, True)
        state += [pv] + carries
    per_group = 1 + heads_per_group

    def body(i, state):
        start = pl.multiple_of((n_past - 1 - i) * ATT_BLOCK, ATT_BLOCK)
        new_state = []
        for g in range(GROUPS_PER_STEP):
            acc = state[g * per_group]
            carries = state[g * per_group + 1:(g + 1) * per_group]
            pv, carries = group_tile(g, kp_ref[0, pl.ds(start, ATT_BLOCK), group_slice(g)],
                                     vp_ref[0, pl.ds(start, ATT_BLOCK), group_slice(g)], carries, False)
            new_state += [acc + pv] + carries
        return tuple(new_state)

    state = lax.fori_loop(0, n_past, body, tuple(state))
    for g in range(GROUPS_PER_STEP):
        o_ref[0, :, group_slice(g)] = state[g * per_group].astype(o_ref.dtype)


def _attn_specs(sk, q_block_offset):
    width = GROUPS_PER_STEP * LANES
    blk = lambda off: pl.BlockSpec((1, ATT_BLOCK, width), lambda bi, g, qi: (bi, qi + off, g))
    whole = pl.BlockSpec((1, sk, width), lambda bi, g, qi: (bi, 0, g))
    return blk(0), blk(q_block_offset), whole


def _sb_attention(q, k, v, u, q_block_offset, n_q_blocks):
    b, sk = k.shape[0], k.shape[1]
    q_spec, diag_spec, whole = _attn_specs(sk, q_block_offset)
    return pl.pallas_call(
        functools.partial(_sb_attn_kernel, q_block_offset=q_block_offset),
        grid=(b, D_MODEL // (GROUPS_PER_STEP * LANES), n_q_blocks),
        in_specs=[q_spec, diag_spec, diag_spec, whole, whole, _resident((ATT_BLOCK, ATT_BLOCK))],
        out_specs=q_spec,
        out_shape=jax.ShapeDtypeStruct((b, n_q_blocks * ATT_BLOCK, D_MODEL), BF16),
        compiler_params=_compiler_params(3),
        name="sb_attention",
    )(q, k, v, k, v, u)


def _diff_attn_kernel(q_ref, kd_ref, vd_ref, kp_ref, vp_ref, lq1_ref, lk1_ref, lq2_ref, lk2_ref, g_ref, o_ref,
                      *, q_block_offset, n_valid_keys, lambda_init):
    n_past = pl.program_id(2) + q_block_offset
    lane = lax.broadcasted_iota(jnp.int32, (1, LANES), 1)
    row = lax.broadcasted_iota(jnp.int32, (ATT_BLOCK, ATT_BLOCK), 0)
    col = lax.broadcasted_iota(jnp.int32, (ATT_BLOCK, ATT_BLOCK), 1)
    visible = ((col // CHUNK) <= (row // CHUNK)) & (col < n_valid_keys)
    in_comp = [lane < DIFF_HEAD_DIM, lane >= DIFF_HEAD_DIM]

    def group_slice(g):
        return slice(g * LANES, (g + 1) * LANES)

    qc = []
    for g in range(GROUPS_PER_STEP):
        q = q_ref[0, :, group_slice(g)]
        qc.append([jnp.where(m, q, jnp.zeros_like(q)) for m in in_comp])

    state = []
    for g in range(GROUPS_PER_STEP):
        kd, vd = kd_ref[0, :, group_slice(g)], vd_ref[0, :, group_slice(g)]
        for c in range(2):
            s = jnp.where(visible, _dot_nt(qc[g][c], kd), MASK_VALUE)
            m = jnp.max(s, axis=1, keepdims=True)
            p = jnp.exp2(s - m)
            state += [m, jnp.sum(p, axis=1, keepdims=True),
                      jnp.dot(p.astype(BF16), vd, preferred_element_type=F32)]

    def body(i, state):
        start = pl.multiple_of((n_past - 1 - i) * ATT_BLOCK, ATT_BLOCK)
        new = []
        for g in range(GROUPS_PER_STEP):
            k = kp_ref[0, pl.ds(start, ATT_BLOCK), group_slice(g)]
            v = vp_ref[0, pl.ds(start, ATT_BLOCK), group_slice(g)]
            for c in range(2):
                m, l, acc = state[3 * (2 * g + c): 3 * (2 * g + c) + 3]
                s = _dot_nt(qc[g][c], k)
                m_new = jnp.maximum(m, jnp.max(s, axis=1, keepdims=True))
                alpha = jnp.exp2(m - m_new)
                p = jnp.exp2(s - m_new)
                new += [m_new, alpha * l + jnp.sum(p, axis=1, keepdims=True),
                        alpha * acc + jnp.dot(p.astype(BF16), v, preferred_element_type=F32)]
        return tuple(new)

    state = lax.fori_loop(0, n_past, body, tuple(state))

    lam = (jnp.exp(jnp.sum(lq1_ref[...] * lk1_ref[...], axis=1, keepdims=True))
           - jnp.exp(jnp.sum(lq2_ref[...] * lk2_ref[...], axis=1, keepdims=True)) + lambda_init)
    for g in range(GROUPS_PER_STEP):
        _, l0, acc0, _, l1, acc1 = state[6 * g: 6 * g + 6]
        o = acc0 / l0 - lam * (acc1 / l1)
        o = o * lax.rsqrt(jnp.mean(o * o, axis=1, keepdims=True) + SUBLN_EPS) * g_ref[...]
        o_ref[0, :, group_slice(g)] = (o * (1.0 - lambda_init)).astype(o_ref.dtype)


def _diff_attention(q, k, v, lam_params, subln_g, q_block_offset, n_q_blocks, n_valid_keys, lambda_init):
    b, sk = k.shape[0], k.shape[1]
    q_spec, diag_spec, whole = _attn_specs(sk, q_block_offset)
    small = [_resident((1, DIFF_HEAD_DIM))] * 4 + [_resident((1, 2 * DIFF_HEAD_DIM))]
    return pl.pallas_call(
        functools.partial(_diff_attn_kernel, q_block_offset=q_block_offset, n_valid_keys=n_valid_keys,
                          lambda_init=lambda_init),
        grid=(b, D_MODEL // (GROUPS_PER_STEP * LANES), n_q_blocks),
        in_specs=[q_spec, diag_spec, diag_spec, whole, whole] + small,
        out_specs=q_spec,
        out_shape=jax.ShapeDtypeStruct((b, n_q_blocks * ATT_BLOCK, D_MODEL), BF16),
        compiler_params=_compiler_params(3),
        name="diff_attention",
    )(q, k, v, k, v, *[p.reshape(1, -1) for p in lam_params], subln_g.reshape(1, -1))


FF_COL_CHUNK = 512


def _layer_norm(z, g, b):
    mu = jnp.mean(z, axis=-1, keepdims=True)
    d = z - mu
    var = jnp.mean(d * d, axis=-1, keepdims=True)
    return d * lax.rsqrt(var + LN_EPS) * g + b


def _post_kernel(o_ref, x_ref, wo_ref, g1_ref, b1_ref, wup_ref, wdown_ref, g2_ref, b2_ref, y_ref, x1_ref, h_ref):
    mix = jnp.dot(o_ref[...], wo_ref[...], preferred_element_type=F32)
    x1_ref[...] = _layer_norm(DEEPNORM_ALPHA * x_ref[...] + mix, g1_ref[...], b1_ref[...])
    xb = x1_ref[...].astype(BF16)
    for c in range(D_FF // FF_COL_CHUNK):
        cols = slice(c * FF_COL_CHUNK, (c + 1) * FF_COL_CHUNK)
        h = jnp.maximum(jnp.dot(xb, wup_ref[:, cols], preferred_element_type=F32), 0.0)
        h_ref[:, cols] = (h * h).astype(BF16)
    ff = jnp.dot(h_ref[...], wdown_ref[...], preferred_element_type=F32)
    y_ref[...] = _layer_norm(DEEPNORM_ALPHA * x1_ref[...] + ff, g2_ref[...], b2_ref[...])


def _post_attention(o, x, wo, g1, b1, wup, wdown, g2, b2, tm):
    n = x.shape[0]
    row_spec = pl.BlockSpec((tm, D_MODEL), lambda i: (i, 0))
    vec = _resident((1, D_MODEL))
    return pl.pallas_call(
        _post_kernel,
        grid=(n // tm,),
        in_specs=[row_spec, row_spec, _resident((D_MODEL, D_MODEL)), vec, vec,
                  _resident((D_MODEL, D_FF)), _resident((D_FF, D_MODEL)), vec, vec],
        out_specs=row_spec,
        out_shape=jax.ShapeDtypeStruct((n, D_MODEL), F32),
        scratch_shapes=[pltpu.VMEM((tm, D_MODEL), F32), pltpu.VMEM((tm, D_FF), BF16)],
        compiler_params=_compiler_params(1),
        name="post_attention",
    )(o, x, wo, g1.reshape(1, -1), b1.reshape(1, -1), wup, wdown, g2.reshape(1, -1), b2.reshape(1, -1))


def _trunk(x, pos, past, weights, tm):
    (sb_w_qkv, sb_w_o, diff_w_qkv, lq1, lk1, lq2, lk2, subln_g, diff_w_o,
     ln1_g, ln1_b, w_up, w_down, ln2_g, ln2_b) = weights
    b, s, _ = x.shape
    n = b * s
    n_q_blocks = -(-s // ATT_BLOCK)
    s_pad = n_q_blocks * ATT_BLOCK
    past_len = 0 if past is None else past[0].shape[2]
    q_block_offset = past_len // ATT_BLOCK
    u = jnp.tril(jnp.ones((ATT_BLOCK, ATT_BLOCK), BF16))
    rope = _rope_tables(jnp.tile(pos, tm // s) if tm > s else pos)

    def seq_layout(t, past_t):
        t = jnp.pad(t.reshape(b, s, D_MODEL), ((0, 0), (0, s_pad - s), (0, 0)))
        return t if past_t is None else jnp.concatenate([past_t.astype(BF16), t], axis=1)

    x = x.reshape(n, D_MODEL)
    new_kv = [[], [], [], []]
    for i in range(DEPTH):
        j = i // N_MIXERS
        is_sb = i % N_MIXERS == 0
        w_qkv = (sb_w_qkv if is_sb else diff_w_qkv)[j].astype(BF16)
        head_dim = SB_HEAD_DIM if is_sb else DIFF_HEAD_DIM
        kf, vf, qb, kb, vb = _qkv_proj(x, w_qkv, None if is_sb else rope, head_dim ** -0.5 * LOG2E, tm)
        kp, vp = (None, None) if past is None else (past[2 * (not is_sb)][j], past[2 * (not is_sb) + 1][j])
        q_seq, k_seq, v_seq = seq_layout(qb, None), seq_layout(kb, kp), seq_layout(vb, vp)
        if is_sb:
            o = _sb_attention(q_seq, k_seq, v_seq, u, q_block_offset, n_q_blocks)
            new_kv[0].append(kf)
            new_kv[1].append(vf)
        else:
            lambda_init = 0.8 - 0.6 * math.exp(-0.3 * i)
            o = _diff_attention(q_seq, k_seq, v_seq, (lq1[j], lk1[j], lq2[j], lk2[j]), subln_g[j],
                                q_block_offset, n_q_blocks, min(s, ATT_BLOCK), lambda_init)
            new_kv[2].append(kf)
            new_kv[3].append(vf)
        o = o[:, :s].reshape(n, D_MODEL)
        w_o = (sb_w_o if is_sb else diff_w_o)[j].astype(BF16)
        x = _post_attention(o, x, w_o, ln1_g[i], ln1_b[i], w_up[i].astype(BF16), w_down[i].astype(BF16),
                            ln2_g[i], ln2_b[i], tm)
    stacked = [jnp.stack(t) for t in new_kv]
    return (x.reshape(b, s, D_MODEL),
            stacked[0].reshape(-1, b, s, SB_HEADS, SB_HEAD_DIM),
            stacked[1].reshape(-1, b, s, SB_HEADS, SB_HEAD_DIM),
            stacked[2].reshape(-1, b, s, DIFF_HEADS, 2, DIFF_HEAD_DIM),
            stacked[3].reshape(-1, b, s, DIFF_HEADS, 2 * DIFF_HEAD_DIM))


def kernel(x_prompt, x_sample, cache_sb_k, cache_sb_v, cache_diff_k, cache_diff_v, sb_w_qkv, sb_w_o, diff_w_qkv,
           diff_lambda_q1, diff_lambda_k1, diff_lambda_q2, diff_lambda_k2, diff_subln_g, diff_w_o, ln1_g, ln1_b,
           mlp_w_up, mlp_w_down, ln2_g, ln2_b):
    weights = (sb_w_qkv, sb_w_o, diff_w_qkv, diff_lambda_q1, diff_lambda_k1, diff_lambda_q2, diff_lambda_k2,
               diff_subln_g, diff_w_o, ln1_g, ln1_b, mlp_w_up, mlp_w_down, ln2_g, ln2_b)
    pos_p = jnp.arange(x_prompt.shape[1], dtype=jnp.int32)
    y_p, sbk_p, sbv_p, dfk_p, dfv_p = _trunk(x_prompt, pos_p, None, weights, tm=512)

    past_len = cache_sb_k.shape[2]
    b = cache_sb_k.shape[1]
    past = tuple(c.reshape(c.shape[0], b, past_len, D_MODEL)
                 for c in (cache_sb_k, cache_sb_v, cache_diff_k, cache_diff_v))
    pos_s = past_len + jnp.arange(x_sample.shape[1], dtype=jnp.int32)
    n_sample = x_sample.shape[0] * x_sample.shape[1]
    y_s, sbk_s, sbv_s, dfk_s, dfv_s = _trunk(x_sample, pos_s, past, weights, tm=n_sample)
    return (y_p, y_s, sbk_p, sbv_p, dfk_p, dfv_p, sbk_s, sbv_s, dfk_s, dfv_s)
```

```python
import functools
import math

import jax
import jax.numpy as jnp
from jax import lax
from jax.experimental import pallas as pl
from jax.experimental.pallas import tpu as pltpu

D_MODEL = 1024
DEPTH = 4
CHUNK = 64
N_MIXERS = 2
SB_HEADS = 16
SB_HEAD_DIM = D_MODEL // SB_HEADS
DIFF_HEADS = 8
DIFF_HEAD_DIM = D_MODEL // (2 * DIFF_HEADS)
ROPE_DIM = DIFF_HEAD_DIM // 4
ROPE_THETA = 500000.0
D_FF = 4 * D_MODEL
LN_EPS = 1e-5
SUBLN_EPS = 1e-5
DEEPNORM_ALPHA = (2 * DEPTH) ** 0.25

LANES = 128
ATT_BLOCK = 256
GROUPS_PER_STEP = 4
LOGIT_CLAMP = 60.0
WEIGHT_FLUSH_EXP = 140.0
VMEM_LIMIT_BYTES = 56 * 1024 * 1024
LOG2E = 1.4426950408889634
LN2 = 0.6931471805599453
MASK_VALUE = -1e30

F32 = jnp.float32
BF16 = jnp.bfloat16


def _compiler_params(n_grid_dims):
    return pltpu.CompilerParams(
        dimension_semantics=("parallel",) * n_grid_dims,
        vmem_limit_bytes=VMEM_LIMIT_BYTES,
    )


def _resident(shape):
    return pl.BlockSpec(shape, lambda *_: (0,) * len(shape), pipeline_mode=pl.Buffered(1))


def _dot_nt(a, b):
    return lax.dot_general(a, b, (((1,), (1,)), ((), ())), preferred_element_type=F32)


QKV_COL_CHUNK = 512


def _qkv_kernel(*refs, rope, q_scale):
    if rope:
        x_ref, w_ref, cos_ref, sin_up_ref, sin_dn_ref, kf_ref, vf_ref, qb_ref, kb_ref, vb_ref = refs
        cos, sin_up, sin_dn = cos_ref[...], sin_up_ref[...], sin_dn_ref[...]
    else:
        x_ref, w_ref, kf_ref, vf_ref, qb_ref, kb_ref, vb_ref = refs
    xb = x_ref[...].astype(BF16)
    for part in range(3):
        for c in range(D_MODEL // QKV_COL_CHUNK):
            lo = c * QKV_COL_CHUNK
            cols = slice(lo, lo + QKV_COL_CHUNK)
            y = jnp.dot(xb, w_ref[:, part * D_MODEL + lo: part * D_MODEL + lo + QKV_COL_CHUNK],
                        preferred_element_type=F32)
            if rope and part < 2:
                pieces = []
                for j in range(QKV_COL_CHUNK // LANES):
                    yj = y[:, j * LANES:(j + 1) * LANES]
                    pieces.append(yj * cos
                                  + pltpu.roll(yj, LANES - ROPE_DIM // 2, 1) * sin_up
                                  + pltpu.roll(yj, ROPE_DIM // 2, 1) * sin_dn)
                y = jnp.concatenate(pieces, axis=1)
            if part == 0:
                qb_ref[:, cols] = (y * q_scale).astype(BF16)
            elif part == 1:
                kf_ref[:, cols] = y
                kb_ref[:, cols] = y.astype(BF16)
            else:
                vf_ref[:, cols] = y
                vb_ref[:, cols] = y.astype(BF16)


def _qkv_proj(x, w, rope_tables, q_scale, tm):
    n = x.shape[0]
    rope = rope_tables is not None
    row_spec = pl.BlockSpec((tm, D_MODEL), lambda i: (i, 0))
    in_specs = [row_spec, _resident((D_MODEL, 3 * D_MODEL))]
    args = [x, w]
    if rope:
        n_pos_tiles = rope_tables[0].shape[0] // tm
        tab_spec = pl.BlockSpec((tm, LANES), lambda i: (i % n_pos_tiles, 0))
        in_specs += [tab_spec] * 3
        args += list(rope_tables)
    out_shape = [jax.ShapeDtypeStruct((n, D_MODEL), F32)] * 2 + [jax.ShapeDtypeStruct((n, D_MODEL), BF16)] * 3
    return pl.pallas_call(
        functools.partial(_qkv_kernel, rope=rope, q_scale=q_scale),
        grid=(n // tm,),
        in_specs=in_specs,
        out_specs=[row_spec] * 5,
        out_shape=out_shape,
        compiler_params=_compiler_params(1),
        name="qkv_rope" if rope else "qkv",
    )(*args)


def _rope_tables(pos):
    half = ROPE_DIM // 2
    inv_freq = ROPE_THETA ** (-jnp.arange(0, ROPE_DIM, 2, dtype=F32) / ROPE_DIM)
    ang = pos.astype(F32)[:, None] * inv_freq[None, :]
    cos, sin = jnp.cos(ang), jnp.sin(ang)
    ones = jnp.ones((pos.shape[0], DIFF_HEAD_DIM - ROPE_DIM), F32)
    zeros_half = jnp.zeros_like(sin)
    zeros_rest = jnp.zeros_like(ones)
    cos_t = jnp.concatenate([cos, cos, ones], axis=1)
    sin_up = jnp.concatenate([-sin, zeros_half, zeros_rest], axis=1)
    sin_dn = jnp.concatenate([zeros_half, sin, zeros_rest], axis=1)
    reps = LANES // DIFF_HEAD_DIM
    return tuple(jnp.tile(t, (1, reps)) for t in (cos_t, sin_up, sin_dn))


def _sb_attn_kernel(q_ref, kd_ref, vd_ref, kp_ref, vp_ref, u_ref, o_ref, *, q_block_offset):
    n_past = pl.program_id(2) + q_block_offset
    lane = lax.broadcasted_iota(jnp.int32, (1, LANES), 1)
    row = lax.broadcasted_iota(jnp.int32, (ATT_BLOCK, ATT_BLOCK), 0)
    col = lax.broadcasted_iota(jnp.int32, (ATT_BLOCK, ATT_BLOCK), 1)
    causal = col < row
    u = u_ref[...]
    heads_per_group = LANES // SB_HEAD_DIM
    in_head = [(lane >= h * SB_HEAD_DIM) & (lane < (h + 1) * SB_HEAD_DIM) for h in range(heads_per_group)]

    def group_slice(g):
        return slice(g * LANES, (g + 1) * LANES)

    qm = []
    for g in range(GROUPS_PER_STEP):
        q = q_ref[0, :, group_slice(g)]
        qm.append([jnp.where(m, q, jnp.zeros_like(q)) for m in in_head])

    def group_tile(g, k, v, carries, masked):
        weights, v_parts, new_carries = [], [], []
        for h in range(heads_per_group):
            z = jnp.minimum(_dot_nt(qm[g][h], k), LOGIT_CLAMP)
            l = jnp.log(1.0 + jnp.exp2(z)) * LOG2E
            if masked:
                l = jnp.where(causal, l, 0.0)
            c = jnp.dot(l.astype(BF16), u, preferred_element_type=F32)
            if carries is not None:
                c = c + carries[h]
            a = jnp.exp2(z - c)
            if masked:
                a = jnp.where(causal, a, 0.0)
            weights.append(a.astype(BF16))
            v_parts.append(jnp.where(in_head[h], v, jnp.zeros_like(v)))
            new_carries.append(c[:, 0:1])
        pv = jnp.dot(jnp.concatenate(weights, axis=1), jnp.concatenate(v_parts, axis=0),
                     preferred_element_type=F32)
        return pv, new_carries

    state = []
    for g in range(GROUPS_PER_STEP):
        pv, carries = group_tile(g, kd_ref[0, :, group_slice(g)], vd_ref[0, :, group_slice(g)], None, True)
        state += [pv] + carries
    per_group = 1 + heads_per_group

    def min_carry(state):
        lowest = None
        for g in range(GROUPS_PER_STEP):
            for c in state[g * per_group + 1:(g + 1) * per_group]:
                lowest = c if lowest is None else jnp.minimum(lowest, c)
        return jnp.min(lowest)

    def body(loop_state):
        i, _, state = loop_state
        start = pl.multiple_of((n_past - 1 - i) * ATT_BLOCK, ATT_BLOCK)
        new_state = []
        for g in range(GROUPS_PER_STEP):
            acc = state[g * per_group]
            carries = state[g * per_group + 1:(g + 1) * per_group]
            pv, carries = group_tile(g, kp_ref[0, pl.ds(start, ATT_BLOCK), group_slice(g)],
                                     vp_ref[0, pl.ds(start, ATT_BLOCK), group_slice(g)], carries, False)
            new_state += [acc + pv] + carries
        return i + 1, min_carry(new_state), tuple(new_state)

    def more_blocks(loop_state):
        i, lowest, _ = loop_state
        return (i < n_past) & (lowest < LOGIT_CLAMP + WEIGHT_FLUSH_EXP)

    _, _, state = lax.while_loop(more_blocks, body, (jnp.int32(0), min_carry(state), tuple(state)))
    for g in range(GROUPS_PER_STEP):
        o_ref[0, :, group_slice(g)] = state[g * per_group].astype(o_ref.dtype)


def _attn_specs(sk, q_block_offset):
    width = GROUPS_PER_STEP * LANES
    blk = lambda off: pl.BlockSpec((1, ATT_BLOCK, width), lambda bi, g, qi: (bi, qi + off, g))
    whole = pl.BlockSpec((1, sk, width), lambda bi, g, qi: (bi, 0, g))
    return blk(0), blk(q_block_offset), whole


def _sb_attention(q, k, v, u, q_block_offset, n_q_blocks):
    b, sk = k.shape[0], k.shape[1]
    q_spec, diag_spec, whole = _attn_specs(sk, q_block_offset)
    return pl.pallas_call(
        functools.partial(_sb_attn_kernel, q_block_offset=q_block_offset),
        grid=(b, D_MODEL // (GROUPS_PER_STEP * LANES), n_q_blocks),
        in_specs=[q_spec, diag_spec, diag_spec, whole, whole, _resident((ATT_BLOCK, ATT_BLOCK))],
        out_specs=q_spec,
        out_shape=jax.ShapeDtypeStruct((b, n_q_blocks * ATT_BLOCK, D_MODEL), BF16),
        compiler_params=_compiler_params(3),
        name="sb_attention",
    )(q, k, v, k, v, u)


def _diff_attn_kernel(q_ref, kd_ref, vd_ref, kp_ref, vp_ref, lq1_ref, lk1_ref, lq2_ref, lk2_ref, g_ref, o_ref,
                      *, q_block_offset, n_valid_keys, lambda_init):
    n_past = pl.program_id(2) + q_block_offset
    lane = lax.broadcasted_iota(jnp.int32, (1, LANES), 1)
    row = lax.broadcasted_iota(jnp.int32, (ATT_BLOCK, ATT_BLOCK), 0)
    col = lax.broadcasted_iota(jnp.int32, (ATT_BLOCK, ATT_BLOCK), 1)
    visible = ((col // CHUNK) <= (row // CHUNK)) & (col < n_valid_keys)
    in_comp = [lane < DIFF_HEAD_DIM, lane >= DIFF_HEAD_DIM]

    def group_slice(g):
        return slice(g * LANES, (g + 1) * LANES)

    qc = []
    for g in range(GROUPS_PER_STEP):
        q = q_ref[0, :, group_slice(g)]
        qc.append([jnp.where(m, q, jnp.zeros_like(q)) for m in in_comp])

    def head_tile(g, k, v, prev, masked):
        ms, alphas, probs, psums = [], [], [], []
        for c in range(2):
            s = _dot_nt(qc[g][c], k)
            if masked:
                s = jnp.where(visible, s, MASK_VALUE)
            m = jnp.max(s, axis=1, keepdims=True)
            if prev is not None:
                m = jnp.maximum(prev[3 * c], m)
                alphas.append(jnp.exp2(prev[3 * c] - m))
            p = jnp.exp2(s - m)
            ms.append(m)
            psums.append(p[:, :LANES] + p[:, LANES:])
            probs.append(p.astype(BF16))
        pv = jnp.dot(jnp.concatenate(probs, axis=0), v, preferred_element_type=F32)
        new = []
        for c in range(2):
            pv_c = pv[c * ATT_BLOCK:(c + 1) * ATT_BLOCK]
            if prev is None:
                new += [ms[c], psums[c], pv_c]
            else:
                new += [ms[c], alphas[c] * prev[3 * c + 1] + psums[c], alphas[c] * prev[3 * c + 2] + pv_c]
        return new

    state = []
    for g in range(GROUPS_PER_STEP):
        state += head_tile(g, kd_ref[0, :, group_slice(g)], vd_ref[0, :, group_slice(g)], None, True)

    def body(i, state):
        start = pl.multiple_of((n_past - 1 - i) * ATT_BLOCK, ATT_BLOCK)
        new = []
        for g in range(GROUPS_PER_STEP):
            new += head_tile(g, kp_ref[0, pl.ds(start, ATT_BLOCK), group_slice(g)],
                             vp_ref[0, pl.ds(start, ATT_BLOCK), group_slice(g)], state[6 * g: 6 * g + 6], False)
        return tuple(new)

    state = lax.fori_loop(0, n_past, body, tuple(state))

    lam = (jnp.exp(jnp.sum(lq1_ref[...] * lk1_ref[...], axis=1, keepdims=True))
           - jnp.exp(jnp.sum(lq2_ref[...] * lk2_ref[...], axis=1, keepdims=True)) + lambda_init)
    for g in range(GROUPS_PER_STEP):
        _, lp0, acc0, _, lp1, acc1 = state[6 * g: 6 * g + 6]
        l0 = jnp.sum(lp0, axis=1, keepdims=True)
        l1 = jnp.sum(lp1, axis=1, keepdims=True)
        o = acc0 / l0 - lam * (acc1 / l1)
        o = o * lax.rsqrt(jnp.mean(o * o, axis=1, keepdims=True) + SUBLN_EPS) * g_ref[...]
        o_ref[0, :, group_slice(g)] = (o * (1.0 - lambda_init)).astype(o_ref.dtype)


def _diff_attention(q, k, v, lam_params, subln_g, q_block_offset, n_q_blocks, n_valid_keys, lambda_init):
    b, sk = k.shape[0], k.shape[1]
    q_spec, diag_spec, whole = _attn_specs(sk, q_block_offset)
    small = [_resident((1, DIFF_HEAD_DIM))] * 4 + [_resident((1, 2 * DIFF_HEAD_DIM))]
    return pl.pallas_call(
        functools.partial(_diff_attn_kernel, q_block_offset=q_block_offset, n_valid_keys=n_valid_keys,
                          lambda_init=lambda_init),
        grid=(b, D_MODEL // (GROUPS_PER_STEP * LANES), n_q_blocks),
        in_specs=[q_spec, diag_spec, diag_spec, whole, whole] + small,
        out_specs=q_spec,
        out_shape=jax.ShapeDtypeStruct((b, n_q_blocks * ATT_BLOCK, D_MODEL), BF16),
        compiler_params=_compiler_params(3),
        name="diff_attention",
    )(q, k, v, k, v, *[p.reshape(1, -1) for p in lam_params], subln_g.reshape(1, -1))


FF_COL_CHUNK = 512


def _layer_norm(z, g, b):
    mu = jnp.mean(z, axis=-1, keepdims=True)
    d = z - mu
    var = jnp.mean(d * d, axis=-1, keepdims=True)
    return d * lax.rsqrt(var + LN_EPS) * g + b


def _post_kernel(o_ref, x_ref, wo_ref, g1_ref, b1_ref, wup_ref, wdown_ref, g2_ref, b2_ref, y_ref, x1_ref, h_ref):
    mix = jnp.dot(o_ref[...], wo_ref[...], preferred_element_type=F32)
    x1_ref[...] = _layer_norm(DEEPNORM_ALPHA * x_ref[...] + mix, g1_ref[...], b1_ref[...])
    xb = x1_ref[...].astype(BF16)
    for c in range(D_FF // FF_COL_CHUNK):
        cols = slice(c * FF_COL_CHUNK, (c + 1) * FF_COL_CHUNK)
        h = jnp.maximum(jnp.dot(xb, wup_ref[:, cols], preferred_element_type=F32), 0.0)
        h_ref[:, cols] = (h * h).astype(BF16)
    ff = jnp.dot(h_ref[...], wdown_ref[...], preferred_element_type=F32)
    y_ref[...] = _layer_norm(DEEPNORM_ALPHA * x1_ref[...] + ff, g2_ref[...], b2_ref[...])


def _post_attention(o, x, wo, g1, b1, wup, wdown, g2, b2, tm):
    n = x.shape[0]
    row_spec = pl.BlockSpec((tm, D_MODEL), lambda i: (i, 0))
    vec = _resident((1, D_MODEL))
    return pl.pallas_call(
        _post_kernel,
        grid=(n // tm,),
        in_specs=[row_spec, row_spec, _resident((D_MODEL, D_MODEL)), vec, vec,
                  _resident((D_MODEL, D_FF)), _resident((D_FF, D_MODEL)), vec, vec],
        out_specs=row_spec,
        out_shape=jax.ShapeDtypeStruct((n, D_MODEL), F32),
        scratch_shapes=[pltpu.VMEM((tm, D_MODEL), F32), pltpu.VMEM((tm, D_FF), BF16)],
        compiler_params=_compiler_params(1),
        name="post_attention",
    )(o, x, wo, g1.reshape(1, -1), b1.reshape(1, -1), wup, wdown, g2.reshape(1, -1), b2.reshape(1, -1))


def _trunk(x, pos, past, weights, tm):
    (sb_w_qkv, sb_w_o, diff_w_qkv, lq1, lk1, lq2, lk2, subln_g, diff_w_o,
     ln1_g, ln1_b, w_up, w_down, ln2_g, ln2_b) = weights
    b, s, _ = x.shape
    n = b * s
    n_q_blocks = -(-s // ATT_BLOCK)
    s_pad = n_q_blocks * ATT_BLOCK
    past_len = 0 if past is None else past[0].shape[2]
    q_block_offset = past_len // ATT_BLOCK
    u = jnp.tril(jnp.ones((ATT_BLOCK, ATT_BLOCK), BF16))
    rope = _rope_tables(jnp.tile(pos, tm // s) if tm > s else pos)

    def seq_layout(t, past_t):
        t = jnp.pad(t.reshape(b, s, D_MODEL), ((0, 0), (0, s_pad - s), (0, 0)))
        return t if past_t is None else jnp.concatenate([past_t.astype(BF16), t], axis=1)

    x = x.reshape(n, D_MODEL)
    new_kv = [[], [], [], []]
    for i in range(DEPTH):
        j = i // N_MIXERS
        is_sb = i % N_MIXERS == 0
        w_qkv = (sb_w_qkv if is_sb else diff_w_qkv)[j].astype(BF16)
        head_dim = SB_HEAD_DIM if is_sb else DIFF_HEAD_DIM
        kf, vf, qb, kb, vb = _qkv_proj(x, w_qkv, None if is_sb else rope, head_dim ** -0.5 * LOG2E, tm)
        kp, vp = (None, None) if past is None else (past[2 * (not is_sb)][j], past[2 * (not is_sb) + 1][j])
        q_seq, k_seq, v_seq = seq_layout(qb, None), seq_layout(kb, kp), seq_layout(vb, vp)
        if is_sb:
            o = _sb_attention(q_seq, k_seq, v_seq, u, q_block_offset, n_q_blocks)
            new_kv[0].append(kf)
            new_kv[1].append(vf)
        else:
            lambda_init = 0.8 - 0.6 * math.exp(-0.3 * i)
            o = _diff_attention(q_seq, k_seq, v_seq, (lq1[j], lk1[j], lq2[j], lk2[j]), subln_g[j],
                                q_block_offset, n_q_blocks, min(s, ATT_BLOCK), lambda_init)
            new_kv[2].append(kf)
            new_kv[3].append(vf)
        o = o[:, :s].reshape(n, D_MODEL)
        w_o = (sb_w_o if is_sb else diff_w_o)[j].astype(BF16)
        x = _post_attention(o, x, w_o, ln1_g[i], ln1_b[i], w_up[i].astype(BF16), w_down[i].astype(BF16),
                            ln2_g[i], ln2_b[i], tm)
    stacked = [jnp.stack(t) for t in new_kv]
    return (x.reshape(b, s, D_MODEL),
            stacked[0].reshape(-1, b, s, SB_HEADS, SB_HEAD_DIM),
            stacked[1].reshape(-1, b, s, SB_HEADS, SB_HEAD_DIM),
            stacked[2].reshape(-1, b, s, DIFF_HEADS, 2, DIFF_HEAD_DIM),
            stacked[3].reshape(-1, b, s, DIFF_HEADS, 2 * DIFF_HEAD_DIM))


def kernel(x_prompt, x_sample, cache_sb_k, cache_sb_v, cache_diff_k, cache_diff_v, sb_w_qkv, sb_w_o, diff_w_qkv,
           diff_lambda_q1, diff_lambda_k1, diff_lambda_q2, diff_lambda_k2, diff_subln_g, diff_w_o, ln1_g, ln1_b,
           mlp_w_up, mlp_w_down, ln2_g, ln2_b):
    weights = (sb_w_qkv, sb_w_o, diff_w_qkv, diff_lambda_q1, diff_lambda_k1, diff_lambda_q2, diff_lambda_k2,
               diff_subln_g, diff_w_o, ln1_g, ln1_b, mlp_w_up, mlp_w_down, ln2_g, ln2_b)
    pos_p = jnp.arange(x_prompt.shape[1], dtype=jnp.int32)
    y_p, sbk_p, sbv_p, dfk_p, dfv_p = _trunk(x_prompt, pos_p, None, weights, tm=512)

    past_len = cache_sb_k.shape[2]
    b = cache_sb_k.shape[1]
    past = tuple(c.reshape(c.shape[0], b, past_len, D_MODEL)
                 for c in (cache_sb_k, cache_sb_v, cache_diff_k, cache_diff_v))
    pos_s = past_len + jnp.arange(x_sample.shape[1], dtype=jnp.int32)
    n_sample = x_sample.shape[0] * x_sample.shape[1]
    y_s, sbk_s, sbv_s, dfk_s, dfv_s = _trunk(x_sample, pos_s, past, weights, tm=n_sample)
    return (y_p, y_s, sbk_p, sbv_p, dfk_p, dfv_p, sbk_s, sbv_s, dfk_s, dfv_s)
```

```python
import functools
import math

import jax
import jax.numpy as jnp
from jax import lax
from jax.experimental import pallas as pl
from jax.experimental.pallas import tpu as pltpu

D_MODEL = 1024
DEPTH = 4
CHUNK = 64
N_MIXERS = 2
SB_HEADS = 16
SB_HEAD_DIM = D_MODEL // SB_HEADS
DIFF_HEADS = 8
DIFF_HEAD_DIM = D_MODEL // (2 * DIFF_HEADS)
ROPE_DIM = DIFF_HEAD_DIM // 4
ROPE_HALF = ROPE_DIM // 2
ROPE_THETA = 500000.0
D_FF = 4 * D_MODEL
LN_EPS = 1e-5
SUBLN_EPS = 1e-5
DEEPNORM_ALPHA = (2 * DEPTH) ** 0.25

LANES = 128
SUBLANES = 8
ATT_BLOCK = 256
GROUPS_PER_STEP = 4
GROUP_WIDTH = GROUPS_PER_STEP * LANES
PROMPT_ROW_TILE = 512
VMEM_LIMIT_BYTES = 56 * 1024 * 1024
LOG2E = 1.4426950408889634
MASK_VALUE = -1e30
LOGIT_CLAMP = 60.0
WEIGHT_FLUSH_EXP = 140.0

F32 = jnp.float32
BF16 = jnp.bfloat16


def _compiler_params(n_grid_dims):
    return pltpu.CompilerParams(
        dimension_semantics=("parallel",) * n_grid_dims,
        vmem_limit_bytes=VMEM_LIMIT_BYTES,
    )


def _resident(shape):
    return pl.BlockSpec(shape, lambda *_: (0,) * len(shape), pipeline_mode=pl.Buffered(1))


def _dot(a, b):
    return jnp.dot(a, b, preferred_element_type=F32)


def _dot_nt(a, b):
    return lax.dot_general(a, b, (((1,), (1,)), ((), ())), preferred_element_type=F32)


def _group_slice(g):
    return slice(g * LANES, (g + 1) * LANES)


QKV_COL_CHUNK = 512


def _rope_lanes(y, cos, sin_up, sin_dn):
    pieces = []
    for j in range(y.shape[1] // LANES):
        yj = y[:, j * LANES:(j + 1) * LANES]
        pieces.append(yj * cos + pltpu.roll(yj, LANES - ROPE_HALF, 1) * sin_up
                      + pltpu.roll(yj, ROPE_HALF, 1) * sin_dn)
    return jnp.concatenate(pieces, axis=1)


def _rope_sublanes(yt, cos_t, sin_t):
    pieces = []
    for base in range(0, yt.shape[0], DIFF_HEAD_DIM):
        y0 = yt[base:base + ROPE_HALF]
        y1 = yt[base + ROPE_HALF:base + ROPE_DIM]
        pieces += [y0 * cos_t - y1 * sin_t, y1 * cos_t + y0 * sin_t, yt[base + ROPE_DIM:base + DIFF_HEAD_DIM]]
    return jnp.concatenate(pieces, axis=0)


def _qkv_kernel(*refs, rope, q_scale):
    if rope:
        x_ref, w_ref, cos_ref, sin_up_ref, sin_dn_ref, kf_ref, vf_ref, qb_ref, kb_ref, vb_ref = refs
        tables = (cos_ref[...], sin_up_ref[...], sin_dn_ref[...])
    else:
        x_ref, w_ref, kf_ref, vf_ref, qb_ref, kb_ref, vb_ref = refs
    xb = x_ref[...].astype(BF16)
    for part in range(3):
        for c in range(D_MODEL // QKV_COL_CHUNK):
            lo = c * QKV_COL_CHUNK
            cols = slice(lo, lo + QKV_COL_CHUNK)
            y = _dot(xb, w_ref[:, part * D_MODEL + lo: part * D_MODEL + lo + QKV_COL_CHUNK])
            if rope and part < 2:
                y = _rope_lanes(y, *tables)
            if part == 0:
                qb_ref[:, cols] = (y * q_scale).astype(BF16)
            elif part == 1:
                kf_ref[:, cols] = y
                kb_ref[:, cols] = y.astype(BF16)
            else:
                vf_ref[:, cols] = y
                vb_ref[:, cols] = y.astype(BF16)


def _qkv_proj(x, w, rope_tables, q_scale, tm):
    n = x.shape[0]
    rope = rope_tables is not None
    row_spec = pl.BlockSpec((tm, D_MODEL), lambda i: (i, 0))
    in_specs = [row_spec, _resident((D_MODEL, 3 * D_MODEL))]
    args = [x, w]
    if rope:
        n_pos_tiles = rope_tables[0].shape[0] // tm
        in_specs += [pl.BlockSpec((tm, LANES), lambda i: (i % n_pos_tiles, 0))] * 3
        args += list(rope_tables)
    out_shape = [jax.ShapeDtypeStruct((n, D_MODEL), F32)] * 2 + [jax.ShapeDtypeStruct((n, D_MODEL), BF16)] * 3
    return pl.pallas_call(
        functools.partial(_qkv_kernel, rope=rope, q_scale=q_scale),
        grid=(n // tm,),
        in_specs=in_specs,
        out_specs=[row_spec] * 5,
        out_shape=out_shape,
        compiler_params=_compiler_params(1),
        name="qkv_rope" if rope else "qkv",
    )(*args)


def _qkv_t_kernel(x_ref, wk_ref, wv_ref, wqt_ref, wvt_ref, cos_ref, sin_up_ref, sin_dn_ref, cos_t_ref, sin_t_ref,
                  kf_ref, vf_ref, kb_ref, qt_ref, vt_ref, *, q_scale):
    xb = x_ref[...].astype(BF16)
    lane_tables = (cos_ref[...], sin_up_ref[...], sin_dn_ref[...])
    cos_t, sin_t = cos_t_ref[...], sin_t_ref[...]
    for c in range(D_MODEL // QKV_COL_CHUNK):
        sl = slice(c * QKV_COL_CHUNK, (c + 1) * QKV_COL_CHUNK)
        k = _rope_lanes(_dot(xb, wk_ref[:, sl]), *lane_tables)
        kf_ref[:, sl] = k
        kb_ref[:, sl] = k.astype(BF16)
        vf_ref[:, sl] = _dot(xb, wv_ref[:, sl])
        qt = _rope_sublanes(_dot_nt(wqt_ref[sl, :], xb), cos_t, sin_t)
        qt_ref[sl, :] = (qt * q_scale).astype(BF16)
        vt_ref[sl, :] = _dot_nt(wvt_ref[sl, :], xb).astype(BF16)


def _qkv_proj_t(x, w, rope_tables, rope_tables_t, q_scale, tm):
    n = x.shape[0]
    wq, wk, wv = (w[:, p * D_MODEL:(p + 1) * D_MODEL] for p in range(3))
    n_pos_tiles = rope_tables[0].shape[0] // tm
    row_spec = pl.BlockSpec((tm, D_MODEL), lambda i: (i, 0))
    col_spec = pl.BlockSpec((D_MODEL, tm), lambda i: (0, i))
    square = _resident((D_MODEL, D_MODEL))
    in_specs = ([row_spec] + [square] * 4
                + [pl.BlockSpec((tm, LANES), lambda i: (i % n_pos_tiles, 0))] * 3
                + [pl.BlockSpec((SUBLANES, tm), lambda i: (0, i % n_pos_tiles))] * 2)
    out_shape = ([jax.ShapeDtypeStruct((n, D_MODEL), F32)] * 2 + [jax.ShapeDtypeStruct((n, D_MODEL), BF16)]
                 + [jax.ShapeDtypeStruct((D_MODEL, n), BF16)] * 2)
    return pl.pallas_call(
        functools.partial(_qkv_t_kernel, q_scale=q_scale),
        grid=(n // tm,),
        in_specs=in_specs,
        out_specs=[row_spec] * 3 + [col_spec] * 2,
        out_shape=out_shape,
        compiler_params=_compiler_params(1),
        name="qkv_rope_t",
    )(x, wk.astype(BF16), wv.astype(BF16), wq.T.astype(BF16), wv.T.astype(BF16), *rope_tables, *rope_tables_t)


def _rope_angles(pos):
    inv_freq = ROPE_THETA ** (-jnp.arange(0, ROPE_DIM, 2, dtype=F32) / ROPE_DIM)
    return pos.astype(F32)[:, None] * inv_freq[None, :]


def _rope_tables(pos):
    ang = _rope_angles(pos)
    cos, sin = jnp.cos(ang), jnp.sin(ang)
    ones = jnp.ones((pos.shape[0], DIFF_HEAD_DIM - ROPE_DIM), F32)
    zeros_half = jnp.zeros_like(sin)
    zeros_rest = jnp.zeros_like(ones)
    cos_t = jnp.concatenate([cos, cos, ones], axis=1)
    sin_up = jnp.concatenate([-sin, zeros_half, zeros_rest], axis=1)
    sin_dn = jnp.concatenate([zeros_half, sin, zeros_rest], axis=1)
    reps = LANES // DIFF_HEAD_DIM
    return tuple(jnp.tile(t, (1, reps)) for t in (cos_t, sin_up, sin_dn))


def _rope_tables_t(pos):
    ang = _rope_angles(pos).T
    return jnp.cos(ang), jnp.sin(ang)


def _sb_attn_kernel(q_ref, kd_ref, vd_ref, kp_ref, vp_ref, u_ref, o_ref, *, n_cached_blocks):
    bq = q_ref.shape[1]
    n_past = pl.program_id(2) + n_cached_blocks
    lane = lax.broadcasted_iota(jnp.int32, (1, LANES), 1)
    row = lax.broadcasted_iota(jnp.int32, (bq, ATT_BLOCK), 0)
    col = lax.broadcasted_iota(jnp.int32, (bq, ATT_BLOCK), 1)
    causal = col < row
    u = u_ref[...]
    heads_per_group = LANES // SB_HEAD_DIM
    in_head = [(lane >= h * SB_HEAD_DIM) & (lane < (h + 1) * SB_HEAD_DIM) for h in range(heads_per_group)]

    qm = []
    for g in range(GROUPS_PER_STEP):
        q = q_ref[0, :, _group_slice(g)]
        qm.append([jnp.where(m, q, jnp.zeros_like(q)) for m in in_head])

    per_group = 1 + heads_per_group
    chains = [(g, h) for g in range(GROUPS_PER_STEP) for h in range(heads_per_group)]

    def key_block(load_k, load_v, state, masked):
        ks = [load_k(g) for g in range(GROUPS_PER_STEP)]
        zs = [jnp.minimum(_dot_nt(qm[g][h], ks[g]), LOGIT_CLAMP) for g, h in chains]
        cs = []
        for (g, h), z in zip(chains, zs):
            l = jnp.log(1.0 + jnp.exp2(z)) * LOG2E
            if masked:
                l = jnp.where(causal, l, 0.0)
            c = _dot(l.astype(BF16), u)
            cs.append(c if state is None else c + state[g * per_group + 1 + h])
        new_state = []
        for g in range(GROUPS_PER_STEP):
            v = load_v(g)
            weights, v_parts = [], []
            for h in range(heads_per_group):
                a = jnp.exp2(zs[g * heads_per_group + h] - cs[g * heads_per_group + h])
                if masked:
                    a = jnp.where(causal, a, 0.0)
                weights.append(a.astype(BF16))
                v_parts.append(jnp.where(in_head[h], v, jnp.zeros_like(v)))
            pv = _dot(jnp.concatenate(weights, axis=1), jnp.concatenate(v_parts, axis=0))
            new_state.append(pv if state is None else state[g * per_group] + pv)
            new_state += [cs[g * heads_per_group + h][:, 0:1] for h in range(heads_per_group)]
        return new_state

    state = key_block(lambda g: kd_ref[0, :, _group_slice(g)], lambda g: vd_ref[0, :, _group_slice(g)], None, True)

    def min_carry(state):
        lowest = None
        for g in range(GROUPS_PER_STEP):
            for c in state[g * per_group + 1:(g + 1) * per_group]:
                lowest = c if lowest is None else jnp.minimum(lowest, c)
        return jnp.min(lowest)

    def body(loop_state):
        i, _, state = loop_state
        start = pl.multiple_of((n_past - 1 - i) * ATT_BLOCK, ATT_BLOCK)
        new_state = key_block(lambda g: kp_ref[0, 0, pl.ds(start, ATT_BLOCK), _group_slice(g)].astype(BF16),
                              lambda g: vp_ref[0, 0, pl.ds(start, ATT_BLOCK), _group_slice(g)].astype(BF16),
                              state, False)
        return i + 1, min_carry(new_state), tuple(new_state)

    def more_blocks(loop_state):
        i, lowest, _ = loop_state
        return (i < n_past) & (lowest < LOGIT_CLAMP + WEIGHT_FLUSH_EXP)

    _, _, state = lax.while_loop(more_blocks, body, (jnp.int32(0), min_carry(state), tuple(state)))
    for g in range(GROUPS_PER_STEP):
        o_ref[0, :, _group_slice(g)] = state[g * per_group].astype(o_ref.dtype)


def _row_block_specs(bq, past, layer):
    q_spec = pl.BlockSpec((1, bq, GROUP_WIDTH), lambda bi, g, qi: (bi, qi, g))
    diag_spec = pl.BlockSpec((1, ATT_BLOCK, GROUP_WIDTH), lambda bi, g, qi: (bi, qi, g))
    past_spec = pl.BlockSpec((1, 1, past.shape[2], GROUP_WIDTH), lambda bi, g, qi: (layer, bi, 0, g))
    return q_spec, diag_spec, past_spec


def _sb_attention(q, k_new, v_new, k_past, v_past, layer, n_cached_blocks, u):
    b, d = q.shape[0], q.shape[2]
    n_q_blocks = k_new.shape[1] // ATT_BLOCK
    bq = q.shape[1] // n_q_blocks
    q_spec, diag_spec, past_spec = _row_block_specs(bq, k_past, layer)
    return pl.pallas_call(
        functools.partial(_sb_attn_kernel, n_cached_blocks=n_cached_blocks),
        grid=(b, d // GROUP_WIDTH, n_q_blocks),
        in_specs=[q_spec, diag_spec, diag_spec, past_spec, past_spec, _resident((ATT_BLOCK, ATT_BLOCK))],
        out_specs=q_spec,
        out_shape=jax.ShapeDtypeStruct(q.shape, BF16),
        compiler_params=_compiler_params(3),
        name="sb_attention",
    )(q, k_new, v_new, k_past, v_past, u)


def _diff_lambda(lq1_ref, lk1_ref, lq2_ref, lk2_ref, lambda_init):
    return (jnp.exp(jnp.sum(lq1_ref[...] * lk1_ref[...], axis=1, keepdims=True))
            - jnp.exp(jnp.sum(lq2_ref[...] * lk2_ref[...], axis=1, keepdims=True)) + lambda_init)


def _diff_attn_kernel(q_ref, kd_ref, vd_ref, kp_ref, vp_ref, lq1_ref, lk1_ref, lq2_ref, lk2_ref, g_ref, o_ref,
                      *, n_cached_blocks, n_valid_keys, lambda_init):
    bq = q_ref.shape[1]
    n_past = pl.program_id(2) + n_cached_blocks
    lane = lax.broadcasted_iota(jnp.int32, (1, LANES), 1)
    row = lax.broadcasted_iota(jnp.int32, (bq, ATT_BLOCK), 0)
    col = lax.broadcasted_iota(jnp.int32, (bq, ATT_BLOCK), 1)
    visible = ((col // CHUNK) <= (row // CHUNK)) & (col < n_valid_keys)
    in_comp = [lane < DIFF_HEAD_DIM, lane >= DIFF_HEAD_DIM]

    qc = []
    for g in range(GROUPS_PER_STEP):
        q = q_ref[0, :, _group_slice(g)]
        qc.append([jnp.where(m, q, jnp.zeros_like(q)) for m in in_comp])

    def head_tile(g, k, v, prev, masked):
        ms, alphas, probs, psums = [], [], [], []
        for c in range(2):
            s = _dot_nt(qc[g][c], k)
            if masked:
                s = jnp.where(visible, s, MASK_VALUE)
            m = jnp.max(s, axis=1, keepdims=True)
            if prev is not None:
                m = jnp.maximum(prev[3 * c], m)
                alphas.append(jnp.exp2(prev[3 * c] - m))
            p = jnp.exp2(s - m)
            ms.append(m)
            psums.append(p[:, :LANES] + p[:, LANES:])
            probs.append(p.astype(BF16))
        pv = _dot(jnp.concatenate(probs, axis=0), v)
        new = []
        for c in range(2):
            pv_c = pv[c * bq:(c + 1) * bq]
            if prev is None:
                new += [ms[c], psums[c], pv_c]
            else:
                new += [ms[c], alphas[c] * prev[3 * c + 1] + psums[c], alphas[c] * prev[3 * c + 2] + pv_c]
        return new

    state = []
    for g in range(GROUPS_PER_STEP):
        state += head_tile(g, kd_ref[0, :, _group_slice(g)], vd_ref[0, :, _group_slice(g)], None, True)

    def body(i, state):
        start = pl.multiple_of((n_past - 1 - i) * ATT_BLOCK, ATT_BLOCK)
        new = []
        for g in range(GROUPS_PER_STEP):
            new += head_tile(g, kp_ref[0, 0, pl.ds(start, ATT_BLOCK), _group_slice(g)].astype(BF16),
                             vp_ref[0, 0, pl.ds(start, ATT_BLOCK), _group_slice(g)].astype(BF16),
                             state[6 * g: 6 * g + 6], False)
        return tuple(new)

    state = lax.fori_loop(0, n_past, body, tuple(state))

    lam = _diff_lambda(lq1_ref, lk1_ref, lq2_ref, lk2_ref, lambda_init)
    for g in range(GROUPS_PER_STEP):
        _, lp0, acc0, _, lp1, acc1 = state[6 * g: 6 * g + 6]
        l0 = jnp.sum(lp0, axis=1, keepdims=True)
        l1 = jnp.sum(lp1, axis=1, keepdims=True)
        o = acc0 / l0 - lam * (acc1 / l1)
        o = o * lax.rsqrt(jnp.mean(o * o, axis=1, keepdims=True) + SUBLN_EPS) * g_ref[...]
        o_ref[0, :, _group_slice(g)] = (o * (1.0 - lambda_init)).astype(o_ref.dtype)


def _diff_attention(q, k_new, v_new, k_past, v_past, layer, n_cached_blocks, n_valid_keys, lam_params, subln_g,
                    lambda_init):
    b, d = q.shape[0], q.shape[2]
    n_q_blocks = k_new.shape[1] // ATT_BLOCK
    bq = q.shape[1] // n_q_blocks
    q_spec, diag_spec, past_spec = _row_block_specs(bq, k_past, layer)
    small = [_resident((1, DIFF_HEAD_DIM))] * 4 + [_resident((1, 2 * DIFF_HEAD_DIM))]
    return pl.pallas_call(
        functools.partial(_diff_attn_kernel, n_cached_blocks=n_cached_blocks, n_valid_keys=n_valid_keys,
                          lambda_init=lambda_init),
        grid=(b, d // GROUP_WIDTH, n_q_blocks),
        in_specs=[q_spec, diag_spec, diag_spec, past_spec, past_spec] + small,
        out_specs=q_spec,
        out_shape=jax.ShapeDtypeStruct(q.shape, BF16),
        compiler_params=_compiler_params(3),
        name="diff_attention",
    )(q, k_new, v_new, k_past, v_past, *[p.reshape(1, -1) for p in lam_params], subln_g.reshape(1, -1))


def _diff_attn_t_kernel(qt_ref, kd_ref, vtd_ref, kp_ref, vtp_ref, lq1_ref, lk1_ref, lq2_ref, lk2_ref, g_ref, o_ref,
                        *, lambda_init):
    n_past = pl.program_id(2)
    key = lax.broadcasted_iota(jnp.int32, (ATT_BLOCK, ATT_BLOCK), 0)
    qry = lax.broadcasted_iota(jnp.int32, (ATT_BLOCK, ATT_BLOCK), 1)
    visible = (key // CHUNK) <= (qry // CHUNK)
    feat = lax.broadcasted_iota(jnp.int32, (LANES, 1), 0)
    in_comp = [feat < DIFF_HEAD_DIM, feat >= DIFF_HEAD_DIM]

    qc = []
    for g in range(GROUPS_PER_STEP):
        qt = qt_ref[_group_slice(g), :]
        qc.append([jnp.where(m, qt, jnp.zeros_like(qt)) for m in in_comp])

    chains = [(g, c) for g in range(GROUPS_PER_STEP) for c in range(2)]

    def key_block(load_k, load_vt, state, masked):
        ks = [load_k(g) for g in range(GROUPS_PER_STEP)]
        scores = [_dot(ks[g], qc[g][c]) for g, c in chains]
        new_state = []
        for n, (g, c) in enumerate(chains):
            s = scores[n]
            if masked:
                s = jnp.where(visible, s, MASK_VALUE)
            m = jnp.max(s, axis=0, keepdims=True)
            if state is not None:
                m = jnp.maximum(state[3 * n], m)
            p = jnp.exp2(s - m)
            psum = jnp.sum(p, axis=0, keepdims=True)
            pv = _dot(load_vt(g), p.astype(BF16))
            if state is None:
                new_state += [m, psum, pv]
            else:
                alpha = jnp.exp2(state[3 * n] - m)
                new_state += [m, alpha * state[3 * n + 1] + psum, alpha * state[3 * n + 2] + pv]
        return new_state

    state = key_block(lambda g: kd_ref[0, :, _group_slice(g)], lambda g: vtd_ref[_group_slice(g), :], None, True)

    def body(i, state):
        start = pl.multiple_of((n_past - 1 - i) * ATT_BLOCK, ATT_BLOCK)
        return tuple(key_block(lambda g: kp_ref[0, 0, pl.ds(start, ATT_BLOCK), _group_slice(g)],
                               lambda g: vtp_ref[_group_slice(g), pl.ds(start, ATT_BLOCK)], state, False))

    state = lax.fori_loop(0, n_past, body, tuple(state))

    lam = _diff_lambda(lq1_ref, lk1_ref, lq2_ref, lk2_ref, lambda_init)
    for g in range(GROUPS_PER_STEP):
        _, l0, acc0, _, l1, acc1 = state[6 * g: 6 * g + 6]
        ot = acc0 / l0 - lam * (acc1 / l1)
        ot = ot * lax.rsqrt(jnp.mean(ot * ot, axis=0, keepdims=True) + SUBLN_EPS) * g_ref[...]
        o_ref[0, :, _group_slice(g)] = (ot * (1.0 - lambda_init)).T.astype(o_ref.dtype)


def _diff_attention_t(qt, k, vt, lam_params, subln_g, lambda_init, b):
    d, n = qt.shape
    s = n // b
    n_q_blocks = s // ATT_BLOCK
    qt_spec = pl.BlockSpec((GROUP_WIDTH, ATT_BLOCK), lambda bi, g, qi: (g, bi * n_q_blocks + qi))
    vt_whole = pl.BlockSpec((GROUP_WIDTH, s), lambda bi, g, qi: (g, bi))
    row_spec, diag_spec, past_spec = _row_block_specs(ATT_BLOCK, k[None], 0)
    small = [_resident((1, DIFF_HEAD_DIM))] * 4 + [_resident((2 * DIFF_HEAD_DIM, 1))]
    return pl.pallas_call(
        functools.partial(_diff_attn_t_kernel, lambda_init=lambda_init),
        grid=(b, d // GROUP_WIDTH, n_q_blocks),
        in_specs=[qt_spec, diag_spec, qt_spec, past_spec, vt_whole] + small,
        out_specs=row_spec,
        out_shape=jax.ShapeDtypeStruct((b, s, d), BF16),
        compiler_params=_compiler_params(3),
        name="diff_attention_t",
    )(qt, k, vt, k[None], vt, *[p.reshape(1, -1) for p in lam_params], subln_g.reshape(-1, 1))


FF_COL_CHUNK = 512


def _layer_norm(z, g, b):
    mu = jnp.mean(z, axis=-1, keepdims=True)
    d = z - mu
    var = jnp.mean(d * d, axis=-1, keepdims=True)
    return d * lax.rsqrt(var + LN_EPS) * g + b


def _post_kernel(o_ref, x_ref, wo_ref, g1_ref, b1_ref, wup_ref, wdown_ref, g2_ref, b2_ref, y_ref, x1_ref, h_ref):
    mix = _dot(o_ref[...], wo_ref[...])
    x1_ref[...] = _layer_norm(DEEPNORM_ALPHA * x_ref[...] + mix, g1_ref[...], b1_ref[...])
    xb = x1_ref[...].astype(BF16)
    for c in range(D_FF // FF_COL_CHUNK):
        cols = slice(c * FF_COL_CHUNK, (c + 1) * FF_COL_CHUNK)
        h = jnp.maximum(_dot(xb, wup_ref[:, cols]), 0.0)
        h_ref[:, cols] = (h * h).astype(BF16)
    ff = _dot(h_ref[...], wdown_ref[...])
    y_ref[...] = _layer_norm(DEEPNORM_ALPHA * x1_ref[...] + ff, g2_ref[...], b2_ref[...])


def _post_attention(o, x, wo, g1, b1, wup, wdown, g2, b2, tm):
    n = x.shape[0]
    row_spec = pl.BlockSpec((tm, D_MODEL), lambda i: (i, 0))
    vec = _resident((1, D_MODEL))
    return pl.pallas_call(
        _post_kernel,
        grid=(n // tm,),
        in_specs=[row_spec, row_spec, _resident((D_MODEL, D_MODEL)), vec, vec,
                  _resident((D_MODEL, D_FF)), _resident((D_FF, D_MODEL)), vec, vec],
        out_specs=row_spec,
        out_shape=jax.ShapeDtypeStruct((n, D_MODEL), F32),
        scratch_shapes=[pltpu.VMEM((tm, D_MODEL), F32), pltpu.VMEM((tm, D_FF), BF16)],
        compiler_params=_compiler_params(1),
        name="post_attention",
    )(o, x, wo, g1.reshape(1, -1), b1.reshape(1, -1), wup, wdown, g2.reshape(1, -1), b2.reshape(1, -1))


def _trunk(x, pos, caches, weights):
    (sb_w_qkv, sb_w_o, diff_w_qkv, lq1, lk1, lq2, lk2, subln_g, diff_w_o,
     ln1_g, ln1_b, w_up, w_down, ln2_g, ln2_b) = weights
    b, s, _ = x.shape
    n = b * s
    is_prompt = caches is None
    tm = PROMPT_ROW_TILE if is_prompt else n
    n_cached_blocks = 0 if is_prompt else caches[0].shape[2] // ATT_BLOCK
    u = jnp.tril(jnp.ones((ATT_BLOCK, ATT_BLOCK), BF16))
    rope = _rope_tables(pos if is_prompt else jnp.tile(pos, b))
    rope_t = _rope_tables_t(pos) if is_prompt else None

    def per_batch(t):
        return t.reshape(b, s, D_MODEL)

    def key_blocks(t):
        t = per_batch(t)
        return t if is_prompt else jnp.pad(t, ((0, 0), (0, ATT_BLOCK - s), (0, 0)))

    x = x.reshape(n, D_MODEL)
    new_kv = [[], [], [], []]
    for i in range(DEPTH):
        j = i // N_MIXERS
        is_sb = i % N_MIXERS == 0
        lambda_init = 0.8 - 0.6 * math.exp(-0.3 * i)
        lam_params = (lq1[j], lk1[j], lq2[j], lk2[j])
        if is_sb:
            kf, vf, qb, kb, vb = _qkv_proj(x, sb_w_qkv[j].astype(BF16), None, SB_HEAD_DIM ** -0.5 * LOG2E, tm)
            k_new, v_new = key_blocks(kb), key_blocks(vb)
            k_past, v_past = (k_new[None], v_new[None]) if is_prompt else caches[0:2]
            o = _sb_attention(per_batch(qb), k_new, v_new, k_past, v_past, 0 if is_prompt else j,
                              n_cached_blocks, u)
        elif is_prompt:
            kf, vf, kb, qt, vt = _qkv_proj_t(x, diff_w_qkv[j], rope, rope_t, DIFF_HEAD_DIM ** -0.5 * LOG2E, tm)
            o = _diff_attention_t(qt, per_batch(kb), vt, lam_params, subln_g[j], lambda_init, b)
        else:
            kf, vf, qb, kb, vb = _qkv_proj(x, diff_w_qkv[j].astype(BF16), rope, DIFF_HEAD_DIM ** -0.5 * LOG2E, tm)
            o = _diff_attention(per_batch(qb), key_blocks(kb), key_blocks(vb), caches[2], caches[3], j,
                                n_cached_blocks, s, lam_params, subln_g[j], lambda_init)
        new_kv[0 if is_sb else 2].append(kf)
        new_kv[1 if is_sb else 3].append(vf)
        w_o = (sb_w_o if is_sb else diff_w_o)[j].astype(BF16)
        x = _post_attention(o.reshape(n, D_MODEL), x, w_o, ln1_g[i], ln1_b[i], w_up[i].astype(BF16),
                            w_down[i].astype(BF16), ln2_g[i], ln2_b[i], tm)
    stacked = [jnp.stack(t) for t in new_kv]
    return (x.reshape(b, s, D_MODEL),
            stacked[0].reshape(-1, b, s, SB_HEADS, SB_HEAD_DIM),
            stacked[1].reshape(-1, b, s, SB_HEADS, SB_HEAD_DIM),
            stacked[2].reshape(-1, b, s, DIFF_HEADS, 2, DIFF_HEAD_DIM),
            stacked[3].reshape(-1, b, s, DIFF_HEADS, 2 * DIFF_HEAD_DIM))


def kernel(x_prompt, x_sample, cache_sb_k, cache_sb_v, cache_diff_k, cache_diff_v, sb_w_qkv, sb_w_o, diff_w_qkv,
           diff_lambda_q1, diff_lambda_k1, diff_lambda_q2, diff_lambda_k2, diff_subln_g, diff_w_o, ln1_g, ln1_b,
           mlp_w_up, mlp_w_down, ln2_g, ln2_b):
    weights = (sb_w_qkv, sb_w_o, diff_w_qkv, diff_lambda_q1, diff_lambda_k1, diff_lambda_q2, diff_lambda_k2,
               diff_subln_g, diff_w_o, ln1_g, ln1_b, mlp_w_up, mlp_w_down, ln2_g, ln2_b)
    pos_p = jnp.arange(x_prompt.shape[1], dtype=jnp.int32)
    y_p, sbk_p, sbv_p, dfk_p, dfv_p = _trunk(x_prompt, pos_p, None, weights)

    past_len = cache_sb_k.shape[2]
    caches = tuple(c.reshape(c.shape[0], c.shape[1], past_len, D_MODEL)
                   for c in (cache_sb_k, cache_sb_v, cache_diff_k, cache_diff_v))
    pos_s = past_len + jnp.arange(x_sample.shape[1], dtype=jnp.int32)
    y_s, sbk_s, sbv_s, dfk_s, dfv_s = _trunk(x_sample, pos_s, caches, weights)
    return (y_p, y_s, sbk_p, sbv_p, dfk_p, dfv_p, sbk_s, sbv_s, dfk_s, dfv_s)
```

```python
import functools
import math

import jax
import jax.numpy as jnp
from jax import lax
from jax.experimental import pallas as pl
from jax.experimental.pallas import tpu as pltpu

D_MODEL = 1024
DEPTH = 4
CHUNK = 64
N_MIXERS = 2
SB_HEADS = 16
SB_HEAD_DIM = D_MODEL // SB_HEADS
DIFF_HEADS = 8
DIFF_HEAD_DIM = D_MODEL // (2 * DIFF_HEADS)
ROPE_DIM = DIFF_HEAD_DIM // 4
ROPE_HALF = ROPE_DIM // 2
ROPE_THETA = 500000.0
D_FF = 4 * D_MODEL
LN_EPS = 1e-5
SUBLN_EPS = 1e-5
DEEPNORM_ALPHA = (2 * DEPTH) ** 0.25

LANES = 128
SUBLANES = 8
ATT_BLOCK = 256
GROUPS_PER_STEP = 4
GROUP_WIDTH = GROUPS_PER_STEP * LANES
PAST_BLOCKS_PER_ITER = 2
PROMPT_ROW_TILE = 512
VMEM_LIMIT_BYTES = 56 * 1024 * 1024
LOG2E = 1.4426950408889634
MASK_VALUE = -1e30
LOGIT_CLAMP = 60.0
WEIGHT_FLUSH_EXP = 140.0

F32 = jnp.float32
BF16 = jnp.bfloat16


def _compiler_params(n_grid_dims):
    return pltpu.CompilerParams(
        dimension_semantics=("parallel",) * n_grid_dims,
        vmem_limit_bytes=VMEM_LIMIT_BYTES,
    )


def _resident(shape):
    return pl.BlockSpec(shape, lambda *_: (0,) * len(shape), pipeline_mode=pl.Buffered(1))


def _dot(a, b):
    return jnp.dot(a, b, preferred_element_type=F32)


def _dot_nt(a, b):
    return lax.dot_general(a, b, (((1,), (1,)), ((), ())), preferred_element_type=F32)


def _group_slice(g):
    return slice(g * LANES, (g + 1) * LANES)


QKV_COL_CHUNK = 512


def _rope_lanes(y, cos, sin_up, sin_dn):
    pieces = []
    for j in range(y.shape[1] // LANES):
        yj = y[:, j * LANES:(j + 1) * LANES]
        pieces.append(yj * cos + pltpu.roll(yj, LANES - ROPE_HALF, 1) * sin_up
                      + pltpu.roll(yj, ROPE_HALF, 1) * sin_dn)
    return jnp.concatenate(pieces, axis=1)


def _rope_sublanes(yt, cos_t, sin_t):
    pieces = []
    for base in range(0, yt.shape[0], DIFF_HEAD_DIM):
        y0 = yt[base:base + ROPE_HALF]
        y1 = yt[base + ROPE_HALF:base + ROPE_DIM]
        pieces += [y0 * cos_t - y1 * sin_t, y1 * cos_t + y0 * sin_t, yt[base + ROPE_DIM:base + DIFF_HEAD_DIM]]
    return jnp.concatenate(pieces, axis=0)


def _qkv_kernel(*refs, rope, q_scale):
    if rope:
        x_ref, w_ref, cos_ref, sin_up_ref, sin_dn_ref, kf_ref, vf_ref, qb_ref, kb_ref, vb_ref = refs
        tables = (cos_ref[...], sin_up_ref[...], sin_dn_ref[...])
    else:
        x_ref, w_ref, kf_ref, vf_ref, qb_ref, kb_ref, vb_ref = refs
    xb = x_ref[...].astype(BF16)
    for part in range(3):
        for c in range(D_MODEL // QKV_COL_CHUNK):
            lo = c * QKV_COL_CHUNK
            cols = slice(lo, lo + QKV_COL_CHUNK)
            y = _dot(xb, w_ref[:, part * D_MODEL + lo: part * D_MODEL + lo + QKV_COL_CHUNK])
            if rope and part < 2:
                y = _rope_lanes(y, *tables)
            if part == 0:
                qb_ref[:, cols] = (y * q_scale).astype(BF16)
            elif part == 1:
                kf_ref[:, cols] = y
                kb_ref[:, cols] = y.astype(BF16)
            else:
                vf_ref[:, cols] = y
                vb_ref[:, cols] = y.astype(BF16)


def _call_with_kv_slabs(kernel, name, grid, in_specs, args, other_out_specs, other_out_shape, slabs, slab, n, tm):
    n_slabs = DEPTH // N_MIXERS
    slab_spec = pl.BlockSpec((None, tm, D_MODEL), lambda i: (slab, i, 0))
    aliases = {}
    if slabs is not None:
        kernel = _skip_refs(kernel, len(args), len(slabs))
        aliases = {len(args) + t: t for t in range(len(slabs))}
        in_specs = list(in_specs) + [pl.BlockSpec(memory_space=pl.ANY)] * len(slabs)
        args = list(args) + list(slabs)
    return pl.pallas_call(
        kernel,
        grid=grid,
        in_specs=in_specs,
        out_specs=[slab_spec] * 2 + list(other_out_specs),
        out_shape=[jax.ShapeDtypeStruct((n_slabs, n, D_MODEL), F32)] * 2 + list(other_out_shape),
        input_output_aliases=aliases,
        compiler_params=_compiler_params(len(grid)),
        name=name,
    )(*args)


def _skip_refs(kernel, first, count):
    def wrapped(*refs):
        return kernel(*refs[:first], *refs[first + count:])
    return wrapped


def _qkv_proj(x, w, rope_tables, q_scale, tm, slabs, slab):
    n = x.shape[0]
    rope = rope_tables is not None
    row_spec = pl.BlockSpec((tm, D_MODEL), lambda i: (i, 0))
    in_specs = [row_spec, _resident((D_MODEL, 3 * D_MODEL))]
    args = [x, w]
    if rope:
        n_pos_tiles = rope_tables[0].shape[0] // tm
        in_specs += [pl.BlockSpec((tm, LANES), lambda i: (i % n_pos_tiles, 0))] * 3
        args += list(rope_tables)
    return _call_with_kv_slabs(
        functools.partial(_qkv_kernel, rope=rope, q_scale=q_scale), "qkv_rope" if rope else "qkv", (n // tm,),
        in_specs, args, [row_spec] * 3, [jax.ShapeDtypeStruct((n, D_MODEL), BF16)] * 3, slabs, slab, n, tm)


def _qkv_t_kernel(x_ref, wk_ref, wv_ref, wqt_ref, wvt_ref, cos_ref, sin_up_ref, sin_dn_ref, cos_t_ref, sin_t_ref,
                  kf_ref, vf_ref, kb_ref, qt_ref, vt_ref, *, q_scale):
    xb = x_ref[...].astype(BF16)
    lane_tables = (cos_ref[...], sin_up_ref[...], sin_dn_ref[...])
    cos_t, sin_t = cos_t_ref[...], sin_t_ref[...]
    for c in range(D_MODEL // QKV_COL_CHUNK):
        sl = slice(c * QKV_COL_CHUNK, (c + 1) * QKV_COL_CHUNK)
        k = _rope_lanes(_dot(xb, wk_ref[:, sl]), *lane_tables)
        kf_ref[:, sl] = k
        kb_ref[:, sl] = k.astype(BF16)
        vf_ref[:, sl] = _dot(xb, wv_ref[:, sl])
        qt = _rope_sublanes(_dot_nt(wqt_ref[sl, :], xb), cos_t, sin_t)
        qt_ref[sl, :] = (qt * q_scale).astype(BF16)
        vt_ref[sl, :] = _dot_nt(wvt_ref[sl, :], xb).astype(BF16)


def _qkv_proj_t(x, w, rope_tables, rope_tables_t, q_scale, tm, slabs, slab):
    n = x.shape[0]
    wq, wk, wv = (w[:, p * D_MODEL:(p + 1) * D_MODEL] for p in range(3))
    n_pos_tiles = rope_tables[0].shape[0] // tm
    row_spec = pl.BlockSpec((tm, D_MODEL), lambda i: (i, 0))
    col_spec = pl.BlockSpec((D_MODEL, tm), lambda i: (0, i))
    square = _resident((D_MODEL, D_MODEL))
    in_specs = ([row_spec] + [square] * 4
                + [pl.BlockSpec((tm, LANES), lambda i: (i % n_pos_tiles, 0))] * 3
                + [pl.BlockSpec((SUBLANES, tm), lambda i: (0, i % n_pos_tiles))] * 2)
    args = [x, wk.astype(BF16), wv.astype(BF16), wq.T.astype(BF16), wv.T.astype(BF16), *rope_tables, *rope_tables_t]
    other_shape = [jax.ShapeDtypeStruct((n, D_MODEL), BF16)] + [jax.ShapeDtypeStruct((D_MODEL, n), BF16)] * 2
    return _call_with_kv_slabs(functools.partial(_qkv_t_kernel, q_scale=q_scale), "qkv_rope_t", (n // tm,),
                               in_specs, args, [row_spec] + [col_spec] * 2, other_shape, slabs, slab, n, tm)


def _rope_angles(pos):
    inv_freq = ROPE_THETA ** (-jnp.arange(0, ROPE_DIM, 2, dtype=F32) / ROPE_DIM)
    return pos.astype(F32)[:, None] * inv_freq[None, :]


def _rope_tables(pos):
    ang = _rope_angles(pos)
    cos, sin = jnp.cos(ang), jnp.sin(ang)
    ones = jnp.ones((pos.shape[0], DIFF_HEAD_DIM - ROPE_DIM), F32)
    zeros_half = jnp.zeros_like(sin)
    zeros_rest = jnp.zeros_like(ones)
    cos_t = jnp.concatenate([cos, cos, ones], axis=1)
    sin_up = jnp.concatenate([-sin, zeros_half, zeros_rest], axis=1)
    sin_dn = jnp.concatenate([zeros_half, sin, zeros_rest], axis=1)
    reps = LANES // DIFF_HEAD_DIM
    return tuple(jnp.tile(t, (1, reps)) for t in (cos_t, sin_up, sin_dn))


def _rope_tables_t(pos):
    ang = _rope_angles(pos).T
    return jnp.cos(ang), jnp.sin(ang)


def _sb_attn_kernel(q_ref, kd_ref, vd_ref, kp_ref, vp_ref, u_ref, o_ref, *, n_cached_blocks):
    bq = q_ref.shape[1]
    n_past = pl.program_id(2) + n_cached_blocks
    lane = lax.broadcasted_iota(jnp.int32, (1, LANES), 1)
    row = lax.broadcasted_iota(jnp.int32, (bq, ATT_BLOCK), 0)
    col = lax.broadcasted_iota(jnp.int32, (bq, ATT_BLOCK), 1)
    causal = col < row
    u = u_ref[...]
    heads_per_group = LANES // SB_HEAD_DIM
    in_head = [(lane >= h * SB_HEAD_DIM) & (lane < (h + 1) * SB_HEAD_DIM) for h in range(heads_per_group)]

    qm = []
    for g in range(GROUPS_PER_STEP):
        q = q_ref[0, :, _group_slice(g)]
        qm.append([jnp.where(m, q, jnp.zeros_like(q)) for m in in_head])

    per_group = 1 + heads_per_group
    chains = [(g, h) for g in range(GROUPS_PER_STEP) for h in range(heads_per_group)]

    def key_block(load_k, load_v, state, masked):
        ks = [load_k(g) for g in range(GROUPS_PER_STEP)]
        zs = [jnp.minimum(_dot_nt(qm[g][h], ks[g]), LOGIT_CLAMP) for g, h in chains]
        cs = []
        for (g, h), z in zip(chains, zs):
            l = jnp.log(1.0 + jnp.exp2(z)) * LOG2E
            if masked:
                l = jnp.where(causal, l, 0.0)
            c = _dot(l.astype(BF16), u)
            cs.append(c if state is None else c + state[g * per_group + 1 + h])
        new_state = []
        for g in range(GROUPS_PER_STEP):
            v = load_v(g)
            weights, v_parts = [], []
            for h in range(heads_per_group):
                a = jnp.exp2(zs[g * heads_per_group + h] - cs[g * heads_per_group + h])
                if masked:
                    a = jnp.where(causal, a, 0.0)
                weights.append(a.astype(BF16))
                v_parts.append(jnp.where(in_head[h], v, jnp.zeros_like(v)))
            pv = _dot(jnp.concatenate(weights, axis=1), jnp.concatenate(v_parts, axis=0))
            new_state.append(pv if state is None else state[g * per_group] + pv)
            new_state += [cs[g * heads_per_group + h][:, 0:1] for h in range(heads_per_group)]
        return new_state

    state = key_block(lambda g: kd_ref[0, :, _group_slice(g)], lambda g: vd_ref[0, :, _group_slice(g)], None, True)

    def min_carry(state):
        lowest = None
        for g in range(GROUPS_PER_STEP):
            for c in state[g * per_group + 1:(g + 1) * per_group]:
                lowest = c if lowest is None else jnp.minimum(lowest, c)
        return jnp.min(lowest)

    def body(loop_state):
        i, _, state = loop_state
        start = pl.multiple_of((n_past - 1 - i) * ATT_BLOCK, ATT_BLOCK)
        new_state = key_block(lambda g: kp_ref[0, 0, pl.ds(start, ATT_BLOCK), _group_slice(g)].astype(BF16),
                              lambda g: vp_ref[0, 0, pl.ds(start, ATT_BLOCK), _group_slice(g)].astype(BF16),
                              state, False)
        return i + 1, min_carry(new_state), tuple(new_state)

    def more_blocks(loop_state):
        i, lowest, _ = loop_state
        return (i < n_past) & (lowest < LOGIT_CLAMP + WEIGHT_FLUSH_EXP)

    _, _, state = lax.while_loop(more_blocks, body, (jnp.int32(0), min_carry(state), tuple(state)))
    for g in range(GROUPS_PER_STEP):
        o_ref[0, :, _group_slice(g)] = state[g * per_group].astype(o_ref.dtype)


def _row_block_specs(bq, past, layer):
    q_spec = pl.BlockSpec((1, bq, GROUP_WIDTH), lambda bi, g, qi: (bi, qi, g))
    diag_spec = pl.BlockSpec((1, ATT_BLOCK, GROUP_WIDTH), lambda bi, g, qi: (bi, qi, g))
    past_spec = pl.BlockSpec((1, 1, past.shape[2], GROUP_WIDTH), lambda bi, g, qi: (layer, bi, 0, g))
    return q_spec, diag_spec, past_spec


def _sb_attention(q, k_new, v_new, k_past, v_past, layer, n_cached_blocks, u):
    b, d = q.shape[0], q.shape[2]
    n_q_blocks = k_new.shape[1] // ATT_BLOCK
    bq = q.shape[1] // n_q_blocks
    q_spec, diag_spec, past_spec = _row_block_specs(bq, k_past, layer)
    return pl.pallas_call(
        functools.partial(_sb_attn_kernel, n_cached_blocks=n_cached_blocks),
        grid=(b, d // GROUP_WIDTH, n_q_blocks),
        in_specs=[q_spec, diag_spec, diag_spec, past_spec, past_spec, _resident((ATT_BLOCK, ATT_BLOCK))],
        out_specs=q_spec,
        out_shape=jax.ShapeDtypeStruct(q.shape, BF16),
        compiler_params=_compiler_params(3),
        name="sb_attention",
    )(q, k_new, v_new, k_past, v_past, u)


def _diff_lambda(lq1_ref, lk1_ref, lq2_ref, lk2_ref, lambda_init):
    return (jnp.exp(jnp.sum(lq1_ref[...] * lk1_ref[...], axis=1, keepdims=True))
            - jnp.exp(jnp.sum(lq2_ref[...] * lk2_ref[...], axis=1, keepdims=True)) + lambda_init)


def _diff_attn_kernel(q_ref, kd_ref, vd_ref, kp_ref, vp_ref, lq1_ref, lk1_ref, lq2_ref, lk2_ref, g_ref, o_ref,
                      *, n_cached_blocks, n_valid_keys, lambda_init):
    bq = q_ref.shape[1]
    n_past = pl.program_id(2) + n_cached_blocks
    lane = lax.broadcasted_iota(jnp.int32, (1, LANES), 1)
    row = lax.broadcasted_iota(jnp.int32, (bq, ATT_BLOCK), 0)
    col = lax.broadcasted_iota(jnp.int32, (bq, ATT_BLOCK), 1)
    visible = ((col // CHUNK) <= (row // CHUNK)) & (col < n_valid_keys)
    in_comp = [lane < DIFF_HEAD_DIM, lane >= DIFF_HEAD_DIM]

    qc = []
    for g in range(GROUPS_PER_STEP):
        q = q_ref[0, :, _group_slice(g)]
        qc.append([jnp.where(m, q, jnp.zeros_like(q)) for m in in_comp])

    def head_tile(g, k, v, prev, masked):
        ms, alphas, probs, psums = [], [], [], []
        for c in range(2):
            s = _dot_nt(qc[g][c], k)
            if masked:
                s = jnp.where(visible, s, MASK_VALUE)
            m = jnp.max(s, axis=1, keepdims=True)
            if prev is not None:
                m = jnp.maximum(prev[3 * c], m)
                alphas.append(jnp.exp2(prev[3 * c] - m))
            p = jnp.exp2(s - m)
            ms.append(m)
            psums.append(p[:, :LANES] + p[:, LANES:])
            probs.append(p.astype(BF16))
        pv = _dot(jnp.concatenate(probs, axis=0), v)
        new = []
        for c in range(2):
            pv_c = pv[c * bq:(c + 1) * bq]
            if prev is None:
                new += [ms[c], psums[c], pv_c]
            else:
                new += [ms[c], alphas[c] * prev[3 * c + 1] + psums[c], alphas[c] * prev[3 * c + 2] + pv_c]
        return new

    state = []
    for g in range(GROUPS_PER_STEP):
        state += head_tile(g, kd_ref[0, :, _group_slice(g)], vd_ref[0, :, _group_slice(g)], None, True)

    def body(i, state):
        start = pl.multiple_of((n_past - 1 - i) * ATT_BLOCK, ATT_BLOCK)
        new = []
        for g in range(GROUPS_PER_STEP):
            new += head_tile(g, kp_ref[0, 0, pl.ds(start, ATT_BLOCK), _group_slice(g)].astype(BF16),
                             vp_ref[0, 0, pl.ds(start, ATT_BLOCK), _group_slice(g)].astype(BF16),
                             state[6 * g: 6 * g + 6], False)
        return tuple(new)

    state = lax.fori_loop(0, n_past, body, tuple(state))

    lam = _diff_lambda(lq1_ref, lk1_ref, lq2_ref, lk2_ref, lambda_init)
    for g in range(GROUPS_PER_STEP):
        _, lp0, acc0, _, lp1, acc1 = state[6 * g: 6 * g + 6]
        l0 = jnp.sum(lp0, axis=1, keepdims=True)
        l1 = jnp.sum(lp1, axis=1, keepdims=True)
        o = acc0 / l0 - lam * (acc1 / l1)
        o = o * lax.rsqrt(jnp.mean(o * o, axis=1, keepdims=True) + SUBLN_EPS) * g_ref[...]
        o_ref[0, :, _group_slice(g)] = (o * (1.0 - lambda_init)).astype(o_ref.dtype)


def _diff_attention(q, k_new, v_new, k_past, v_past, layer, n_cached_blocks, n_valid_keys, lam_params, subln_g,
                    lambda_init):
    b, d = q.shape[0], q.shape[2]
    n_q_blocks = k_new.shape[1] // ATT_BLOCK
    bq = q.shape[1] // n_q_blocks
    q_spec, diag_spec, past_spec = _row_block_specs(bq, k_past, layer)
    small = [_resident((1, DIFF_HEAD_DIM))] * 4 + [_resident((1, 2 * DIFF_HEAD_DIM))]
    return pl.pallas_call(
        functools.partial(_diff_attn_kernel, n_cached_blocks=n_cached_blocks, n_valid_keys=n_valid_keys,
                          lambda_init=lambda_init),
        grid=(b, d // GROUP_WIDTH, n_q_blocks),
        in_specs=[q_spec, diag_spec, diag_spec, past_spec, past_spec] + small,
        out_specs=q_spec,
        out_shape=jax.ShapeDtypeStruct(q.shape, BF16),
        compiler_params=_compiler_params(3),
        name="diff_attention",
    )(q, k_new, v_new, k_past, v_past, *[p.reshape(1, -1) for p in lam_params], subln_g.reshape(1, -1))


def _diff_attn_t_kernel(qt_ref, kd_ref, vtd_ref, kp_ref, vtp_ref, lq1_ref, lk1_ref, lq2_ref, lk2_ref, g_ref, o_ref,
                        *, lambda_init):
    n_past = pl.program_id(2)
    key = lax.broadcasted_iota(jnp.int32, (ATT_BLOCK, ATT_BLOCK), 0)
    qry = lax.broadcasted_iota(jnp.int32, (ATT_BLOCK, ATT_BLOCK), 1)
    visible = (key // CHUNK) <= (qry // CHUNK)
    feat = lax.broadcasted_iota(jnp.int32, (LANES, 1), 0)
    in_comp = [feat < DIFF_HEAD_DIM, feat >= DIFF_HEAD_DIM]

    qc = []
    for g in range(GROUPS_PER_STEP):
        qt = qt_ref[_group_slice(g), :]
        qc.append([jnp.where(m, qt, jnp.zeros_like(qt)) for m in in_comp])

    chains = [(g, c) for g in range(GROUPS_PER_STEP) for c in range(2)]

    def key_block(load_k, load_vt, state, masked):
        ks = [load_k(g) for g in range(GROUPS_PER_STEP)]
        scores = [_dot(ks[g], qc[g][c]) for g, c in chains]
        new_state = []
        for n, (g, c) in enumerate(chains):
            s = scores[n]
            if masked:
                s = jnp.where(visible, s, MASK_VALUE)
            m = jnp.max(s, axis=0, keepdims=True)
            if state is not None:
                m = jnp.maximum(state[3 * n], m)
            p = jnp.exp2(s - m)
            psum = jnp.sum(p, axis=0, keepdims=True)
            pv = _dot(load_vt(g), p.astype(BF16))
            if state is None:
                new_state += [m, psum, pv]
            else:
                alpha = jnp.exp2(state[3 * n] - m)
                new_state += [m, alpha * state[3 * n + 1] + psum, alpha * state[3 * n + 2] + pv]
        return new_state

    state = key_block(lambda g: kd_ref[0, :, _group_slice(g)], lambda g: vtd_ref[_group_slice(g), :], None, True)

    def past_keys(first_block, n_blocks, state):
        start = pl.multiple_of(first_block * ATT_BLOCK, ATT_BLOCK)
        size = n_blocks * ATT_BLOCK
        return tuple(key_block(lambda g: kp_ref[0, 0, pl.ds(start, size), _group_slice(g)],
                               lambda g: vtp_ref[_group_slice(g), pl.ds(start, size)], state, False))

    state = lax.fori_loop(0, n_past % PAST_BLOCKS_PER_ITER, lambda i, st: past_keys(n_past - 1, 1, st), tuple(state))
    n_iters = n_past // PAST_BLOCKS_PER_ITER
    state = lax.fori_loop(0, n_iters, lambda i, st: past_keys((n_iters - 1 - i) * PAST_BLOCKS_PER_ITER,
                                                              PAST_BLOCKS_PER_ITER, st), state)

    lam = _diff_lambda(lq1_ref, lk1_ref, lq2_ref, lk2_ref, lambda_init)
    for g in range(GROUPS_PER_STEP):
        _, l0, acc0, _, l1, acc1 = state[6 * g: 6 * g + 6]
        ot = acc0 / l0 - lam * (acc1 / l1)
        ot = ot * lax.rsqrt(jnp.mean(ot * ot, axis=0, keepdims=True) + SUBLN_EPS) * g_ref[...]
        o_ref[0, :, _group_slice(g)] = (ot * (1.0 - lambda_init)).T.astype(o_ref.dtype)


def _diff_attention_t(qt, k, vt, lam_params, subln_g, lambda_init, b):
    d, n = qt.shape
    s = n // b
    n_q_blocks = s // ATT_BLOCK
    qt_spec = pl.BlockSpec((GROUP_WIDTH, ATT_BLOCK), lambda bi, g, qi: (g, bi * n_q_blocks + qi))
    vt_whole = pl.BlockSpec((GROUP_WIDTH, s), lambda bi, g, qi: (g, bi))
    row_spec, diag_spec, past_spec = _row_block_specs(ATT_BLOCK, k[None], 0)
    small = [_resident((1, DIFF_HEAD_DIM))] * 4 + [_resident((2 * DIFF_HEAD_DIM, 1))]
    return pl.pallas_call(
        functools.partial(_diff_attn_t_kernel, lambda_init=lambda_init),
        grid=(b, d // GROUP_WIDTH, n_q_blocks),
        in_specs=[qt_spec, diag_spec, qt_spec, past_spec, vt_whole] + small,
        out_specs=row_spec,
        out_shape=jax.ShapeDtypeStruct((b, s, d), BF16),
        compiler_params=_compiler_params(3),
        name="diff_attention_t",
    )(qt, k, vt, k[None], vt, *[p.reshape(1, -1) for p in lam_params], subln_g.reshape(-1, 1))


FF_COL_CHUNK = 512


def _layer_norm(z, g, b):
    mu = jnp.mean(z, axis=-1, keepdims=True)
    d = z - mu
    var = jnp.mean(d * d, axis=-1, keepdims=True)
    return d * lax.rsqrt(var + LN_EPS) * g + b


def _post_kernel(o_ref, x_ref, wo_ref, g1_ref, b1_ref, wup_ref, wdown_ref, g2_ref, b2_ref, y_ref, x1_ref, h_ref):
    mix = _dot(o_ref[...], wo_ref[...])
    x1_ref[...] = _layer_norm(DEEPNORM_ALPHA * x_ref[...] + mix, g1_ref[...], b1_ref[...])
    xb = x1_ref[...].astype(BF16)
    for c in range(D_FF // FF_COL_CHUNK):
        cols = slice(c * FF_COL_CHUNK, (c + 1) * FF_COL_CHUNK)
        h = jnp.maximum(_dot(xb, wup_ref[:, cols]), 0.0)
        h_ref[:, cols] = (h * h).astype(BF16)
    ff = _dot(h_ref[...], wdown_ref[...])
    y_ref[...] = _layer_norm(DEEPNORM_ALPHA * x1_ref[...] + ff, g2_ref[...], b2_ref[...])


def _post_attention(o, x, wo, g1, b1, wup, wdown, g2, b2, tm):
    n = x.shape[0]
    row_spec = pl.BlockSpec((tm, D_MODEL), lambda i: (i, 0))
    vec = _resident((1, D_MODEL))
    return pl.pallas_call(
        _post_kernel,
        grid=(n // tm,),
        in_specs=[row_spec, row_spec, _resident((D_MODEL, D_MODEL)), vec, vec,
                  _resident((D_MODEL, D_FF)), _resident((D_FF, D_MODEL)), vec, vec],
        out_specs=row_spec,
        out_shape=jax.ShapeDtypeStruct((n, D_MODEL), F32),
        scratch_shapes=[pltpu.VMEM((tm, D_MODEL), F32), pltpu.VMEM((tm, D_FF), BF16)],
        compiler_params=_compiler_params(1),
        name="post_attention",
    )(o, x, wo, g1.reshape(1, -1), b1.reshape(1, -1), wup, wdown, g2.reshape(1, -1), b2.reshape(1, -1))


def _trunk(x, pos, caches, weights):
    (sb_w_qkv, sb_w_o, diff_w_qkv, lq1, lk1, lq2, lk2, subln_g, diff_w_o,
     ln1_g, ln1_b, w_up, w_down, ln2_g, ln2_b) = weights
    b, s, _ = x.shape
    n = b * s
    is_prompt = caches is None
    tm = PROMPT_ROW_TILE if is_prompt else n
    n_cached_blocks = 0 if is_prompt else caches[0].shape[2] // ATT_BLOCK
    u = jnp.tril(jnp.ones((ATT_BLOCK, ATT_BLOCK), BF16))
    rope = _rope_tables(pos if is_prompt else jnp.tile(pos, b))
    rope_t = _rope_tables_t(pos) if is_prompt else None

    def per_batch(t):
        return t.reshape(b, s, D_MODEL)

    def key_blocks(t):
        t = per_batch(t)
        return t if is_prompt else jnp.pad(t, ((0, 0), (0, ATT_BLOCK - s), (0, 0)))

    x = x.reshape(n, D_MODEL)
    sb_kv = diff_kv = None
    for i in range(DEPTH):
        j = i // N_MIXERS
        is_sb = i % N_MIXERS == 0
        lambda_init = 0.8 - 0.6 * math.exp(-0.3 * i)
        lam_params = (lq1[j], lk1[j], lq2[j], lk2[j])
        if is_sb:
            *sb_kv, qb, kb, vb = _qkv_proj(x, sb_w_qkv[j].astype(BF16), None, SB_HEAD_DIM ** -0.5 * LOG2E, tm,
                                           sb_kv, j)
            k_new, v_new = key_blocks(kb), key_blocks(vb)
            k_past, v_past = (k_new[None], v_new[None]) if is_prompt else caches[0:2]
            o = _sb_attention(per_batch(qb), k_new, v_new, k_past, v_past, 0 if is_prompt else j,
                              n_cached_blocks, u)
        elif is_prompt:
            *diff_kv, kb, qt, vt = _qkv_proj_t(x, diff_w_qkv[j], rope, rope_t, DIFF_HEAD_DIM ** -0.5 * LOG2E, tm,
                                               diff_kv, j)
            o = _diff_attention_t(qt, per_batch(kb), vt, lam_params, subln_g[j], lambda_init, b)
        else:
            *diff_kv, qb, kb, vb = _qkv_proj(x, diff_w_qkv[j].astype(BF16), rope, DIFF_HEAD_DIM ** -0.5 * LOG2E, tm,
                                             diff_kv, j)
            o = _diff_attention(per_batch(qb), key_blocks(kb), key_blocks(vb), caches[2], caches[3], j,
                                n_cached_blocks, s, lam_params, subln_g[j], lambda_init)
        w_o = (sb_w_o if is_sb else diff_w_o)[j].astype(BF16)
        x = _post_attention(o.reshape(n, D_MODEL), x, w_o, ln1_g[i], ln1_b[i], w_up[i].astype(BF16),
                            w_down[i].astype(BF16), ln2_g[i], ln2_b[i], tm)
    return (x.reshape(b, s, D_MODEL),
            sb_kv[0].reshape(-1, b, s, SB_HEADS, SB_HEAD_DIM),
            sb_kv[1].reshape(-1, b, s, SB_HEADS, SB_HEAD_DIM),
            diff_kv[0].reshape(-1, b, s, DIFF_HEADS, 2, DIFF_HEAD_DIM),
            diff_kv[1].reshape(-1, b, s, DIFF_HEADS, 2 * DIFF_HEAD_DIM))


def kernel(x_prompt, x_sample, cache_sb_k, cache_sb_v, cache_diff_k, cache_diff_v, sb_w_qkv, sb_w_o, diff_w_qkv,
           diff_lambda_q1, diff_lambda_k1, diff_lambda_q2, diff_lambda_k2, diff_subln_g, diff_w_o, ln1_g, ln1_b,
           mlp_w_up, mlp_w_down, ln2_g, ln2_b):
    weights = (sb_w_qkv, sb_w_o, diff_w_qkv, diff_lambda_q1, diff_lambda_k1, diff_lambda_q2, diff_lambda_k2,
               diff_subln_g, diff_w_o, ln1_g, ln1_b, mlp_w_up, mlp_w_down, ln2_g, ln2_b)
    pos_p = jnp.arange(x_prompt.shape[1], dtype=jnp.int32)
    y_p, sbk_p, sbv_p, dfk_p, dfv_p = _trunk(x_prompt, pos_p, None, weights)

    past_len = cache_sb_k.shape[2]
    caches = tuple(c.reshape(c.shape[0], c.shape[1], past_len, D_MODEL)
                   for c in (cache_sb_k, cache_sb_v, cache_diff_k, cache_diff_v))
    pos_s = past_len + jnp.arange(x_sample.shape[1], dtype=jnp.int32)
    y_s, sbk_s, sbv_s, dfk_s, dfv_s = _trunk(x_sample, pos_s, caches, weights)
    return (y_p, y_s, sbk_p, sbv_p, dfk_p, dfv_p, sbk_s, sbv_s, dfk_s, dfv_s)
```

```python
import functools
import math

import jax
import jax.numpy as jnp
from jax import lax
from jax.experimental import pallas as pl
from jax.experimental.pallas import tpu as pltpu

D_MODEL = 1024
DEPTH = 4
CHUNK = 64
N_MIXERS = 2
SB_HEADS = 16
SB_HEAD_DIM = D_MODEL // SB_HEADS
DIFF_HEADS = 8
DIFF_HEAD_DIM = D_MODEL // (2 * DIFF_HEADS)
ROPE_DIM = DIFF_HEAD_DIM // 4
ROPE_HALF = ROPE_DIM // 2
ROPE_THETA = 500000.0
D_FF = 4 * D_MODEL
LN_EPS = 1e-5
SUBLN_EPS = 1e-5
DEEPNORM_ALPHA = (2 * DEPTH) ** 0.25

LANES = 128
SUBLANES = 8
ATT_BLOCK = 256
GROUPS_PER_STEP = 4
GROUP_WIDTH = GROUPS_PER_STEP * LANES
PAST_BLOCKS_PER_ITER = 2
PROMPT_ROW_TILE = 512
VMEM_LIMIT_BYTES = 56 * 1024 * 1024
LOG2E = 1.4426950408889634
MASK_VALUE = -1e30
LOGIT_CLAMP = 60.0
WEIGHT_FLUSH_EXP = 140.0

F32 = jnp.float32
BF16 = jnp.bfloat16


def _compiler_params(n_grid_dims):
    return pltpu.CompilerParams(
        dimension_semantics=("parallel",) * n_grid_dims,
        vmem_limit_bytes=VMEM_LIMIT_BYTES,
    )


def _resident(shape):
    return pl.BlockSpec(shape, lambda *_: (0,) * len(shape), pipeline_mode=pl.Buffered(1))


def _dot(a, b):
    return jnp.dot(a, b, preferred_element_type=F32)


def _dot_nt(a, b):
    return lax.dot_general(a, b, (((1,), (1,)), ((), ())), preferred_element_type=F32)


def _group_slice(g):
    return slice(g * LANES, (g + 1) * LANES)


QKV_COL_CHUNK = 512


def _rope_lanes(y, cos, sin_up, sin_dn):
    pieces = []
    for j in range(y.shape[1] // LANES):
        yj = y[:, j * LANES:(j + 1) * LANES]
        pieces.append(yj * cos + pltpu.roll(yj, LANES - ROPE_HALF, 1) * sin_up
                      + pltpu.roll(yj, ROPE_HALF, 1) * sin_dn)
    return jnp.concatenate(pieces, axis=1)


def _rope_sublanes(yt, cos_t, sin_t):
    pieces = []
    for base in range(0, yt.shape[0], DIFF_HEAD_DIM):
        y0 = yt[base:base + ROPE_HALF]
        y1 = yt[base + ROPE_HALF:base + ROPE_DIM]
        pieces += [y0 * cos_t - y1 * sin_t, y1 * cos_t + y0 * sin_t, yt[base + ROPE_DIM:base + DIFF_HEAD_DIM]]
    return jnp.concatenate(pieces, axis=0)


def _qkv_kernel(*refs, rope, q_scale):
    if rope:
        x_ref, w_ref, cos_ref, sin_up_ref, sin_dn_ref, kf_ref, vf_ref, qb_ref, kb_ref, vb_ref = refs
        tables = (cos_ref[...], sin_up_ref[...], sin_dn_ref[...])
    else:
        x_ref, w_ref, kf_ref, vf_ref, qb_ref, kb_ref, vb_ref = refs
    xb = x_ref[...].astype(BF16)
    for part in range(3):
        for c in range(D_MODEL // QKV_COL_CHUNK):
            lo = c * QKV_COL_CHUNK
            cols = slice(lo, lo + QKV_COL_CHUNK)
            y = _dot(xb, w_ref[:, part * D_MODEL + lo: part * D_MODEL + lo + QKV_COL_CHUNK])
            if rope and part < 2:
                y = _rope_lanes(y, *tables)
            if part == 0:
                qb_ref[:, cols] = (y * q_scale).astype(BF16)
            elif part == 1:
                kf_ref[:, cols] = y
                kb_ref[:, cols] = y.astype(BF16)
            else:
                vf_ref[:, cols] = y
                vb_ref[:, cols] = y.astype(BF16)


N_SLABS = DEPTH // N_MIXERS


def _slab_rows(n, tm, slab):
    return (jax.ShapeDtypeStruct((N_SLABS, n, D_MODEL), F32),
            pl.BlockSpec((None, tm, D_MODEL), lambda i: (slab, i, 0)))


def _slab_cols(b, s, tm, slab):
    tiles = s // tm
    return (jax.ShapeDtypeStruct((N_SLABS, b, D_MODEL, s), F32),
            pl.BlockSpec((None, None, D_MODEL, tm), lambda i: (slab, i // tiles, 0, i % tiles)))


def _call_with_slabs(kernel, name, grid, in_specs, args, slab_outs, other_out_specs, other_out_shape, slabs):
    aliases = {}
    if slabs is not None:
        kernel = _skip_refs(kernel, len(args), len(slabs))
        aliases = {len(args) + t: t for t in range(len(slabs))}
        in_specs = list(in_specs) + [pl.BlockSpec(memory_space=pl.ANY)] * len(slabs)
        args = list(args) + list(slabs)
    return pl.pallas_call(
        kernel,
        grid=grid,
        in_specs=in_specs,
        out_specs=[spec for _, spec in slab_outs] + list(other_out_specs),
        out_shape=[shape for shape, _ in slab_outs] + list(other_out_shape),
        input_output_aliases=aliases,
        compiler_params=_compiler_params(len(grid)),
        name=name,
    )(*args)


def _skip_refs(kernel, first, count):
    def wrapped(*refs):
        return kernel(*refs[:first], *refs[first + count:])
    return wrapped


def _qkv_proj(x, w, rope_tables, q_scale, tm, slabs, slab):
    n = x.shape[0]
    rope = rope_tables is not None
    row_spec = pl.BlockSpec((tm, D_MODEL), lambda i: (i, 0))
    in_specs = [row_spec, _resident((D_MODEL, 3 * D_MODEL))]
    args = [x, w]
    if rope:
        n_pos_tiles = rope_tables[0].shape[0] // tm
        in_specs += [pl.BlockSpec((tm, LANES), lambda i: (i % n_pos_tiles, 0))] * 3
        args += list(rope_tables)
    return _call_with_slabs(
        functools.partial(_qkv_kernel, rope=rope, q_scale=q_scale), "qkv_rope" if rope else "qkv", (n // tm,),
        in_specs, args, [_slab_rows(n, tm, slab)] * 2, [row_spec] * 3,
        [jax.ShapeDtypeStruct((n, D_MODEL), BF16)] * 3, slabs)


def _qkv_sb_t_kernel(x_ref, wq_ref, wv_ref, wkt_ref, wvt_ref, ktf_ref, vtf_ref, qb_ref, ktb_ref, vb_ref, *, q_scale):
    xb = x_ref[...].astype(BF16)
    for c in range(D_MODEL // QKV_COL_CHUNK):
        sl = slice(c * QKV_COL_CHUNK, (c + 1) * QKV_COL_CHUNK)
        qb_ref[:, sl] = (_dot(xb, wq_ref[:, sl]) * q_scale).astype(BF16)
        vb_ref[:, sl] = _dot(xb, wv_ref[:, sl]).astype(BF16)
        kt = _dot_nt(wkt_ref[sl, :], xb)
        ktf_ref[sl, :] = kt
        ktb_ref[sl, :] = kt.astype(BF16)
        vtf_ref[sl, :] = _dot_nt(wvt_ref[sl, :], xb)


def _qkv_diff_t_kernel(x_ref, wk_ref, wv_ref, wqt_ref, wkt_ref, wvt_ref, cos_ref, sin_up_ref, sin_dn_ref,
                       cos_t_ref, sin_t_ref, ktf_ref, vf_ref, kb_ref, qt_ref, vt_ref, *, q_scale):
    xb = x_ref[...].astype(BF16)
    lane_tables = (cos_ref[...], sin_up_ref[...], sin_dn_ref[...])
    cos_t, sin_t = cos_t_ref[...], sin_t_ref[...]
    for c in range(D_MODEL // QKV_COL_CHUNK):
        sl = slice(c * QKV_COL_CHUNK, (c + 1) * QKV_COL_CHUNK)
        kb_ref[:, sl] = _rope_lanes(_dot(xb, wk_ref[:, sl]), *lane_tables).astype(BF16)
        vf_ref[:, sl] = _dot(xb, wv_ref[:, sl])
        ktf_ref[sl, :] = _rope_sublanes(_dot_nt(wkt_ref[sl, :], xb), cos_t, sin_t)
        qt = _rope_sublanes(_dot_nt(wqt_ref[sl, :], xb), cos_t, sin_t)
        qt_ref[sl, :] = (qt * q_scale).astype(BF16)
        vt_ref[sl, :] = _dot_nt(wvt_ref[sl, :], xb).astype(BF16)


def _qkv_proj_sb_t(x, w, q_scale, tm, b, slabs, slab):
    n = x.shape[0]
    wq, wk, wv = (w[:, p * D_MODEL:(p + 1) * D_MODEL] for p in range(3))
    row_spec = pl.BlockSpec((tm, D_MODEL), lambda i: (i, 0))
    col_spec = pl.BlockSpec((D_MODEL, tm), lambda i: (0, i))
    rows_bf16 = jax.ShapeDtypeStruct((n, D_MODEL), BF16)
    args = [x, wq.astype(BF16), wv.astype(BF16), wk.T.astype(BF16), wv.T.astype(BF16)]
    return _call_with_slabs(
        functools.partial(_qkv_sb_t_kernel, q_scale=q_scale), "qkv_sb_t", (n // tm,),
        [row_spec] + [_resident((D_MODEL, D_MODEL))] * 4, args, [_slab_cols(b, n // b, tm, slab)] * 2,
        [row_spec, col_spec, row_spec], [rows_bf16, jax.ShapeDtypeStruct((D_MODEL, n), BF16), rows_bf16], slabs)


def _qkv_proj_diff_t(x, w, rope_tables, rope_tables_t, q_scale, tm, b, slabs, slab):
    n = x.shape[0]
    wq, wk, wv = (w[:, p * D_MODEL:(p + 1) * D_MODEL] for p in range(3))
    n_pos_tiles = rope_tables[0].shape[0] // tm
    row_spec = pl.BlockSpec((tm, D_MODEL), lambda i: (i, 0))
    col_spec = pl.BlockSpec((D_MODEL, tm), lambda i: (0, i))
    in_specs = ([row_spec] + [_resident((D_MODEL, D_MODEL))] * 5
                + [pl.BlockSpec((tm, LANES), lambda i: (i % n_pos_tiles, 0))] * 3
                + [pl.BlockSpec((SUBLANES, tm), lambda i: (0, i % n_pos_tiles))] * 2)
    args = [x, wk.astype(BF16), wv.astype(BF16), wq.T.astype(BF16), wk.T.astype(BF16), wv.T.astype(BF16),
            *rope_tables, *rope_tables_t]
    other_shape = [jax.ShapeDtypeStruct((n, D_MODEL), BF16)] + [jax.ShapeDtypeStruct((D_MODEL, n), BF16)] * 2
    return _call_with_slabs(
        functools.partial(_qkv_diff_t_kernel, q_scale=q_scale), "qkv_diff_t", (n // tm,), in_specs, args,
        [_slab_cols(b, n // b, tm, slab), _slab_rows(n, tm, slab)], [row_spec] + [col_spec] * 2, other_shape, slabs)


def _rope_angles(pos):
    inv_freq = ROPE_THETA ** (-jnp.arange(0, ROPE_DIM, 2, dtype=F32) / ROPE_DIM)
    return pos.astype(F32)[:, None] * inv_freq[None, :]


def _rope_tables(pos):
    ang = _rope_angles(pos)
    cos, sin = jnp.cos(ang), jnp.sin(ang)
    ones = jnp.ones((pos.shape[0], DIFF_HEAD_DIM - ROPE_DIM), F32)
    zeros_half = jnp.zeros_like(sin)
    zeros_rest = jnp.zeros_like(ones)
    cos_t = jnp.concatenate([cos, cos, ones], axis=1)
    sin_up = jnp.concatenate([-sin, zeros_half, zeros_rest], axis=1)
    sin_dn = jnp.concatenate([zeros_half, sin, zeros_rest], axis=1)
    reps = LANES // DIFF_HEAD_DIM
    return tuple(jnp.tile(t, (1, reps)) for t in (cos_t, sin_up, sin_dn))


def _rope_tables_t(pos):
    ang = _rope_angles(pos).T
    return jnp.cos(ang), jnp.sin(ang)


def _sb_attn_kernel(q_ref, kd_ref, vd_ref, kp_ref, vp_ref, u_ref, o_ref, *, n_cached_blocks, keys_transposed):
    bq = q_ref.shape[1]
    n_past = pl.program_id(2) + n_cached_blocks
    lane = lax.broadcasted_iota(jnp.int32, (1, LANES), 1)
    row = lax.broadcasted_iota(jnp.int32, (bq, ATT_BLOCK), 0)
    col = lax.broadcasted_iota(jnp.int32, (bq, ATT_BLOCK), 1)
    causal = col < row
    u = u_ref[...]
    heads_per_group = LANES // SB_HEAD_DIM
    in_head = [(lane >= h * SB_HEAD_DIM) & (lane < (h + 1) * SB_HEAD_DIM) for h in range(heads_per_group)]

    qm = []
    for g in range(GROUPS_PER_STEP):
        q = q_ref[0, :, _group_slice(g)]
        qm.append([jnp.where(m, q, jnp.zeros_like(q)) for m in in_head])

    per_group = 1 + heads_per_group
    chains = [(g, h) for g in range(GROUPS_PER_STEP) for h in range(heads_per_group)]

    def key_block(load_k, load_v, state, masked):
        ks = [load_k(g) for g in range(GROUPS_PER_STEP)]
        logits = _dot if keys_transposed else _dot_nt
        zs = [jnp.minimum(logits(qm[g][h], ks[g]), LOGIT_CLAMP) for g, h in chains]
        cs = []
        for (g, h), z in zip(chains, zs):
            l = jnp.log(1.0 + jnp.exp2(z)) * LOG2E
            if masked:
                l = jnp.where(causal, l, 0.0)
            c = _dot(l.astype(BF16), u)
            cs.append(c if state is None else c + state[g * per_group + 1 + h])
        new_state = []
        for g in range(GROUPS_PER_STEP):
            v = load_v(g)
            weights, v_parts = [], []
            for h in range(heads_per_group):
                a = jnp.exp2(zs[g * heads_per_group + h] - cs[g * heads_per_group + h])
                if masked:
                    a = jnp.where(causal, a, 0.0)
                weights.append(a.astype(BF16))
                v_parts.append(jnp.where(in_head[h], v, jnp.zeros_like(v)))
            pv = _dot(jnp.concatenate(weights, axis=1), jnp.concatenate(v_parts, axis=0))
            new_state.append(pv if state is None else state[g * per_group] + pv)
            new_state += [cs[g * heads_per_group + h][:, 0:1] for h in range(heads_per_group)]
        return new_state

    if keys_transposed:
        diag_k = lambda g: kd_ref[_group_slice(g), :]
        past_k = lambda g, start: kp_ref[_group_slice(g), pl.ds(start, ATT_BLOCK)]
    else:
        diag_k = lambda g: kd_ref[0, :, _group_slice(g)]
        past_k = lambda g, start: kp_ref[0, 0, pl.ds(start, ATT_BLOCK), _group_slice(g)].astype(BF16)
    state = key_block(diag_k, lambda g: vd_ref[0, :, _group_slice(g)], None, True)

    def min_carry(state):
        lowest = None
        for g in range(GROUPS_PER_STEP):
            for c in state[g * per_group + 1:(g + 1) * per_group]:
                lowest = c if lowest is None else jnp.minimum(lowest, c)
        return jnp.min(lowest)

    def body(loop_state):
        i, _, state = loop_state
        start = pl.multiple_of((n_past - 1 - i) * ATT_BLOCK, ATT_BLOCK)
        new_state = key_block(lambda g: past_k(g, start),
                              lambda g: vp_ref[0, 0, pl.ds(start, ATT_BLOCK), _group_slice(g)].astype(BF16),
                              state, False)
        return i + 1, min_carry(new_state), tuple(new_state)

    def more_blocks(loop_state):
        i, lowest, _ = loop_state
        return (i < n_past) & (lowest < LOGIT_CLAMP + WEIGHT_FLUSH_EXP)

    _, _, state = lax.while_loop(more_blocks, body, (jnp.int32(0), min_carry(state), tuple(state)))
    for g in range(GROUPS_PER_STEP):
        o_ref[0, :, _group_slice(g)] = state[g * per_group].astype(o_ref.dtype)


def _row_block_specs(bq, past, layer):
    q_spec = pl.BlockSpec((1, bq, GROUP_WIDTH), lambda bi, g, qi: (bi, qi, g))
    diag_spec = pl.BlockSpec((1, ATT_BLOCK, GROUP_WIDTH), lambda bi, g, qi: (bi, qi, g))
    past_spec = pl.BlockSpec((1, 1, past.shape[2], GROUP_WIDTH), lambda bi, g, qi: (layer, bi, 0, g))
    return q_spec, diag_spec, past_spec


def _sb_attention(q, k_new, v_new, k_past, v_past, layer, n_cached_blocks, u):
    b, d = q.shape[0], q.shape[2]
    n_q_blocks = k_new.shape[1] // ATT_BLOCK
    bq = q.shape[1] // n_q_blocks
    q_spec, diag_spec, past_spec = _row_block_specs(bq, k_past, layer)
    return pl.pallas_call(
        functools.partial(_sb_attn_kernel, n_cached_blocks=n_cached_blocks, keys_transposed=False),
        grid=(b, d // GROUP_WIDTH, n_q_blocks),
        in_specs=[q_spec, diag_spec, diag_spec, past_spec, past_spec, _resident((ATT_BLOCK, ATT_BLOCK))],
        out_specs=q_spec,
        out_shape=jax.ShapeDtypeStruct(q.shape, BF16),
        compiler_params=_compiler_params(3),
        name="sb_attention",
    )(q, k_new, v_new, k_past, v_past, u)


def _sb_attention_kt(q, kt, v, u):
    b, s, d = q.shape
    n_q_blocks = s // ATT_BLOCK
    q_spec, diag_spec, past_spec = _row_block_specs(ATT_BLOCK, v[None], 0)
    kt_diag = pl.BlockSpec((GROUP_WIDTH, ATT_BLOCK), lambda bi, g, qi: (g, bi * n_q_blocks + qi))
    kt_whole = pl.BlockSpec((GROUP_WIDTH, s), lambda bi, g, qi: (g, bi))
    return pl.pallas_call(
        functools.partial(_sb_attn_kernel, n_cached_blocks=0, keys_transposed=True),
        grid=(b, d // GROUP_WIDTH, n_q_blocks),
        in_specs=[q_spec, kt_diag, diag_spec, kt_whole, past_spec, _resident((ATT_BLOCK, ATT_BLOCK))],
        out_specs=q_spec,
        out_shape=jax.ShapeDtypeStruct(q.shape, BF16),
        compiler_params=_compiler_params(3),
        name="sb_attention_kt",
    )(q, kt, v, kt, v[None], u)


def _diff_lambda(lq1_ref, lk1_ref, lq2_ref, lk2_ref, lambda_init):
    return (jnp.exp(jnp.sum(lq1_ref[...] * lk1_ref[...], axis=1, keepdims=True))
            - jnp.exp(jnp.sum(lq2_ref[...] * lk2_ref[...], axis=1, keepdims=True)) + lambda_init)


def _diff_attn_kernel(q_ref, kd_ref, vd_ref, kp_ref, vp_ref, lq1_ref, lk1_ref, lq2_ref, lk2_ref, g_ref, o_ref,
                      *, n_cached_blocks, n_valid_keys, lambda_init):
    bq = q_ref.shape[1]
    n_past = pl.program_id(2) + n_cached_blocks
    lane = lax.broadcasted_iota(jnp.int32, (1, LANES), 1)
    row = lax.broadcasted_iota(jnp.int32, (bq, ATT_BLOCK), 0)
    col = lax.broadcasted_iota(jnp.int32, (bq, ATT_BLOCK), 1)
    visible = ((col // CHUNK) <= (row // CHUNK)) & (col < n_valid_keys)
    in_comp = [lane < DIFF_HEAD_DIM, lane >= DIFF_HEAD_DIM]

    qc = []
    for g in range(GROUPS_PER_STEP):
        q = q_ref[0, :, _group_slice(g)]
        qc.append([jnp.where(m, q, jnp.zeros_like(q)) for m in in_comp])

    def head_tile(g, k, v, prev, masked):
        ms, alphas, probs, psums = [], [], [], []
        for c in range(2):
            s = _dot_nt(qc[g][c], k)
            if masked:
                s = jnp.where(visible, s, MASK_VALUE)
            m = jnp.max(s, axis=1, keepdims=True)
            if prev is not None:
                m = jnp.maximum(prev[3 * c], m)
                alphas.append(jnp.exp2(prev[3 * c] - m))
            p = jnp.exp2(s - m)
            ms.append(m)
            psums.append(p[:, :LANES] + p[:, LANES:])
            probs.append(p.astype(BF16))
        pv = _dot(jnp.concatenate(probs, axis=0), v)
        new = []
        for c in range(2):
            pv_c = pv[c * bq:(c + 1) * bq]
            if prev is None:
                new += [ms[c], psums[c], pv_c]
            else:
                new += [ms[c], alphas[c] * prev[3 * c + 1] + psums[c], alphas[c] * prev[3 * c + 2] + pv_c]
        return new

    state = []
    for g in range(GROUPS_PER_STEP):
        state += head_tile(g, kd_ref[0, :, _group_slice(g)], vd_ref[0, :, _group_slice(g)], None, True)

    def body(i, state):
        start = pl.multiple_of((n_past - 1 - i) * ATT_BLOCK, ATT_BLOCK)
        new = []
        for g in range(GROUPS_PER_STEP):
            new += head_tile(g, kp_ref[0, 0, pl.ds(start, ATT_BLOCK), _group_slice(g)].astype(BF16),
                             vp_ref[0, 0, pl.ds(start, ATT_BLOCK), _group_slice(g)].astype(BF16),
                             state[6 * g: 6 * g + 6], False)
        return tuple(new)

    state = lax.fori_loop(0, n_past, body, tuple(state))

    lam = _diff_lambda(lq1_ref, lk1_ref, lq2_ref, lk2_ref, lambda_init)
    for g in range(GROUPS_PER_STEP):
        _, lp0, acc0, _, lp1, acc1 = state[6 * g: 6 * g + 6]
        l0 = jnp.sum(lp0, axis=1, keepdims=True)
        l1 = jnp.sum(lp1, axis=1, keepdims=True)
        o = acc0 / l0 - lam * (acc1 / l1)
        o = o * lax.rsqrt(jnp.mean(o * o, axis=1, keepdims=True) + SUBLN_EPS) * g_ref[...]
        o_ref[0, :, _group_slice(g)] = (o * (1.0 - lambda_init)).astype(o_ref.dtype)


def _diff_attention(q, k_new, v_new, k_past, v_past, layer, n_cached_blocks, n_valid_keys, lam_params, subln_g,
                    lambda_init):
    b, d = q.shape[0], q.shape[2]
    n_q_blocks = k_new.shape[1] // ATT_BLOCK
    bq = q.shape[1] // n_q_blocks
    q_spec, diag_spec, past_spec = _row_block_specs(bq, k_past, layer)
    small = [_resident((1, DIFF_HEAD_DIM))] * 4 + [_resident((1, 2 * DIFF_HEAD_DIM))]
    return pl.pallas_call(
        functools.partial(_diff_attn_kernel, n_cached_blocks=n_cached_blocks, n_valid_keys=n_valid_keys,
                          lambda_init=lambda_init),
        grid=(b, d // GROUP_WIDTH, n_q_blocks),
        in_specs=[q_spec, diag_spec, diag_spec, past_spec, past_spec] + small,
        out_specs=q_spec,
        out_shape=jax.ShapeDtypeStruct(q.shape, BF16),
        compiler_params=_compiler_params(3),
        name="diff_attention",
    )(q, k_new, v_new, k_past, v_past, *[p.reshape(1, -1) for p in lam_params], subln_g.reshape(1, -1))


def _diff_attn_t_kernel(qt_ref, kd_ref, vtd_ref, kp_ref, vtp_ref, lq1_ref, lk1_ref, lq2_ref, lk2_ref, g_ref, o_ref,
                        *, lambda_init):
    n_past = pl.program_id(2)
    key = lax.broadcasted_iota(jnp.int32, (ATT_BLOCK, ATT_BLOCK), 0)
    qry = lax.broadcasted_iota(jnp.int32, (ATT_BLOCK, ATT_BLOCK), 1)
    visible = (key // CHUNK) <= (qry // CHUNK)
    feat = lax.broadcasted_iota(jnp.int32, (LANES, 1), 0)
    in_comp = [feat < DIFF_HEAD_DIM, feat >= DIFF_HEAD_DIM]

    qc = []
    for g in range(GROUPS_PER_STEP):
        qt = qt_ref[_group_slice(g), :]
        qc.append([jnp.where(m, qt, jnp.zeros_like(qt)) for m in in_comp])

    chains = [(g, c) for g in range(GROUPS_PER_STEP) for c in range(2)]

    def key_block(load_k, load_vt, state, masked):
        ks = [load_k(g) for g in range(GROUPS_PER_STEP)]
        scores = [_dot(ks[g], qc[g][c]) for g, c in chains]
        new_state = []
        for n, (g, c) in enumerate(chains):
            s = scores[n]
            if masked:
                s = jnp.where(visible, s, MASK_VALUE)
            m = jnp.max(s, axis=0, keepdims=True)
            if state is not None:
                m = jnp.maximum(state[3 * n], m)
            p = jnp.exp2(s - m)
            psum = jnp.sum(p, axis=0, keepdims=True)
            pv = _dot(load_vt(g), p.astype(BF16))
            if state is None:
                new_state += [m, psum, pv]
            else:
                alpha = jnp.exp2(state[3 * n] - m)
                new_state += [m, alpha * state[3 * n + 1] + psum, alpha * state[3 * n + 2] + pv]
        return new_state

    state = key_block(lambda g: kd_ref[0, :, _group_slice(g)], lambda g: vtd_ref[_group_slice(g), :], None, True)

    def past_keys(first_block, n_blocks, state):
        start = pl.multiple_of(first_block * ATT_BLOCK, ATT_BLOCK)
        size = n_blocks * ATT_BLOCK
        return tuple(key_block(lambda g: kp_ref[0, 0, pl.ds(start, size), _group_slice(g)],
                               lambda g: vtp_ref[_group_slice(g), pl.ds(start, size)], state, False))

    state = lax.fori_loop(0, n_past % PAST_BLOCKS_PER_ITER, lambda i, st: past_keys(n_past - 1, 1, st), tuple(state))
    n_iters = n_past // PAST_BLOCKS_PER_ITER
    state = lax.fori_loop(0, n_iters, lambda i, st: past_keys((n_iters - 1 - i) * PAST_BLOCKS_PER_ITER,
                                                              PAST_BLOCKS_PER_ITER, st), state)

    lam = _diff_lambda(lq1_ref, lk1_ref, lq2_ref, lk2_ref, lambda_init)
    for g in range(GROUPS_PER_STEP):
        _, l0, acc0, _, l1, acc1 = state[6 * g: 6 * g + 6]
        ot = acc0 / l0 - lam * (acc1 / l1)
        ot = ot * lax.rsqrt(jnp.mean(ot * ot, axis=0, keepdims=True) + SUBLN_EPS) * g_ref[...]
        o_ref[0, :, _group_slice(g)] = (ot * (1.0 - lambda_init)).T.astype(o_ref.dtype)


def _diff_attention_t(qt, k, vt, lam_params, subln_g, lambda_init, b):
    d, n = qt.shape
    s = n // b
    n_q_blocks = s // ATT_BLOCK
    qt_spec = pl.BlockSpec((GROUP_WIDTH, ATT_BLOCK), lambda bi, g, qi: (g, bi * n_q_blocks + qi))
    vt_whole = pl.BlockSpec((GROUP_WIDTH, s), lambda bi, g, qi: (g, bi))
    row_spec, diag_spec, past_spec = _row_block_specs(ATT_BLOCK, k[None], 0)
    small = [_resident((1, DIFF_HEAD_DIM))] * 4 + [_resident((2 * DIFF_HEAD_DIM, 1))]
    return pl.pallas_call(
        functools.partial(_diff_attn_t_kernel, lambda_init=lambda_init),
        grid=(b, d // GROUP_WIDTH, n_q_blocks),
        in_specs=[qt_spec, diag_spec, qt_spec, past_spec, vt_whole] + small,
        out_specs=row_spec,
        out_shape=jax.ShapeDtypeStruct((b, s, d), BF16),
        compiler_params=_compiler_params(3),
        name="diff_attention_t",
    )(qt, k, vt, k[None], vt, *[p.reshape(1, -1) for p in lam_params], subln_g.reshape(-1, 1))


FF_COL_CHUNK = 512


def _layer_norm(z, g, b):
    mu = jnp.mean(z, axis=-1, keepdims=True)
    d = z - mu
    var = jnp.mean(d * d, axis=-1, keepdims=True)
    return d * lax.rsqrt(var + LN_EPS) * g + b


def _post_kernel(o_ref, x_ref, wo_ref, g1_ref, b1_ref, wup_ref, wdown_ref, g2_ref, b2_ref, y_ref, x1_ref, h_ref):
    mix = _dot(o_ref[...], wo_ref[...])
    x1_ref[...] = _layer_norm(DEEPNORM_ALPHA * x_ref[...] + mix, g1_ref[...], b1_ref[...])
    xb = x1_ref[...].astype(BF16)
    for c in range(D_FF // FF_COL_CHUNK):
        cols = slice(c * FF_COL_CHUNK, (c + 1) * FF_COL_CHUNK)
        h = jnp.maximum(_dot(xb, wup_ref[:, cols]), 0.0)
        h_ref[:, cols] = (h * h).astype(BF16)
    ff = _dot(h_ref[...], wdown_ref[...])
    y_ref[...] = _layer_norm(DEEPNORM_ALPHA * x1_ref[...] + ff, g2_ref[...], b2_ref[...])


def _post_attention(o, x, wo, g1, b1, wup, wdown, g2, b2, tm):
    n = x.shape[0]
    row_spec = pl.BlockSpec((tm, D_MODEL), lambda i: (i, 0))
    vec = _resident((1, D_MODEL))
    return pl.pallas_call(
        _post_kernel,
        grid=(n // tm,),
        in_specs=[row_spec, row_spec, _resident((D_MODEL, D_MODEL)), vec, vec,
                  _resident((D_MODEL, D_FF)), _resident((D_FF, D_MODEL)), vec, vec],
        out_specs=row_spec,
        out_shape=jax.ShapeDtypeStruct((n, D_MODEL), F32),
        scratch_shapes=[pltpu.VMEM((tm, D_MODEL), F32), pltpu.VMEM((tm, D_FF), BF16)],
        compiler_params=_compiler_params(1),
        name="post_attention",
    )(o, x, wo, g1.reshape(1, -1), b1.reshape(1, -1), wup, wdown, g2.reshape(1, -1), b2.reshape(1, -1))


def _trunk(x, pos, caches, weights):
    (sb_w_qkv, sb_w_o, diff_w_qkv, lq1, lk1, lq2, lk2, subln_g, diff_w_o,
     ln1_g, ln1_b, w_up, w_down, ln2_g, ln2_b) = weights
    b, s, _ = x.shape
    n = b * s
    is_prompt = caches is None
    tm = PROMPT_ROW_TILE if is_prompt else n
    n_cached_blocks = 0 if is_prompt else caches[0].shape[2] // ATT_BLOCK
    u = jnp.tril(jnp.ones((ATT_BLOCK, ATT_BLOCK), BF16))
    rope = _rope_tables(pos if is_prompt else jnp.tile(pos, b))
    rope_t = _rope_tables_t(pos) if is_prompt else None

    def per_batch(t):
        return t.reshape(b, s, D_MODEL)

    def key_blocks(t):
        t = per_batch(t)
        return t if is_prompt else jnp.pad(t, ((0, 0), (0, ATT_BLOCK - s), (0, 0)))

    x = x.reshape(n, D_MODEL)
    sb_kv = diff_kv = None
    for i in range(DEPTH):
        j = i // N_MIXERS
        is_sb = i % N_MIXERS == 0
        lambda_init = 0.8 - 0.6 * math.exp(-0.3 * i)
        lam_params = (lq1[j], lk1[j], lq2[j], lk2[j])
        if is_sb and is_prompt:
            *sb_kv, qb, ktb, vb = _qkv_proj_sb_t(x, sb_w_qkv[j], SB_HEAD_DIM ** -0.5 * LOG2E, tm, b, sb_kv, j)
            o = _sb_attention_kt(per_batch(qb), ktb, per_batch(vb), u)
        elif is_sb:
            *sb_kv, qb, kb, vb = _qkv_proj(x, sb_w_qkv[j].astype(BF16), None, SB_HEAD_DIM ** -0.5 * LOG2E, tm,
                                           sb_kv, j)
            o = _sb_attention(per_batch(qb), key_blocks(kb), key_blocks(vb), caches[0], caches[1], j,
                              n_cached_blocks, u)
        elif is_prompt:
            *diff_kv, kb, qt, vt = _qkv_proj_diff_t(x, diff_w_qkv[j], rope, rope_t, DIFF_HEAD_DIM ** -0.5 * LOG2E,
                                                    tm, b, diff_kv, j)
            o = _diff_attention_t(qt, per_batch(kb), vt, lam_params, subln_g[j], lambda_init, b)
        else:
            *diff_kv, qb, kb, vb = _qkv_proj(x, diff_w_qkv[j].astype(BF16), rope, DIFF_HEAD_DIM ** -0.5 * LOG2E, tm,
                                             diff_kv, j)
            o = _diff_attention(per_batch(qb), key_blocks(kb), key_blocks(vb), caches[2], caches[3], j,
                                n_cached_blocks, s, lam_params, subln_g[j], lambda_init)
        w_o = (sb_w_o if is_sb else diff_w_o)[j].astype(BF16)
        x = _post_attention(o.reshape(n, D_MODEL), x, w_o, ln1_g[i], ln1_b[i], w_up[i].astype(BF16),
                            w_down[i].astype(BF16), ln2_g[i], ln2_b[i], tm)
    def heads_last(t, *head_shape):
        if t.ndim == 3:
            return t.reshape(-1, b, s, *head_shape)
        return jnp.moveaxis(t.reshape(-1, b, *head_shape, s), -1, 2)

    return (x.reshape(b, s, D_MODEL),
            heads_last(sb_kv[0], SB_HEADS, SB_HEAD_DIM),
            heads_last(sb_kv[1], SB_HEADS, SB_HEAD_DIM),
            heads_last(diff_kv[0], DIFF_HEADS, 2, DIFF_HEAD_DIM),
            heads_last(diff_kv[1], DIFF_HEADS, 2 * DIFF_HEAD_DIM))


def kernel(x_prompt, x_sample, cache_sb_k, cache_sb_v, cache_diff_k, cache_diff_v, sb_w_qkv, sb_w_o, diff_w_qkv,
           diff_lambda_q1, diff_lambda_k1, diff_lambda_q2, diff_lambda_k2, diff_subln_g, diff_w_o, ln1_g, ln1_b,
           mlp_w_up, mlp_w_down, ln2_g, ln2_b):
    weights = (sb_w_qkv, sb_w_o, diff_w_qkv, diff_lambda_q1, diff_lambda_k1, diff_lambda_q2, diff_lambda_k2,
               diff_subln_g, diff_w_o, ln1_g, ln1_b, mlp_w_up, mlp_w_down, ln2_g, ln2_b)
    pos_p = jnp.arange(x_prompt.shape[1], dtype=jnp.int32)
    y_p, sbk_p, sbv_p, dfk_p, dfv_p = _trunk(x_prompt, pos_p, None, weights)

    past_len = cache_sb_k.shape[2]
    caches = tuple(c.reshape(c.shape[0], c.shape[1], past_len, D_MODEL)
                   for c in (cache_sb_k, cache_sb_v, cache_diff_k, cache_diff_v))
    pos_s = past_len + jnp.arange(x_sample.shape[1], dtype=jnp.int32)
    y_s, sbk_s, sbv_s, dfk_s, dfv_s = _trunk(x_sample, pos_s, caches, weights)
    return (y_p, y_s, sbk_p, sbv_p, dfk_p, dfv_p, sbk_s, sbv_s, dfk_s, dfv_s)
```
